```python
import math
import jax, jax.numpy as jnp
from jax import lax
import numpy as np

D_MODEL = 2048
BATCH = 2
SEQ = 4096
DEPTH = 4

HEAD_DIM = 128
A_PAIRS = ((128, 1), (512, 4), (2048, 16))
A_GROUPS = len(A_PAIRS)
A_HEADS = D_MODEL // 256
A_OUT = A_HEADS * HEAD_DIM
A_COLS = A_GROUPS * 3 * A_HEADS * HEAD_DIM
B_CHUNK = 128
B_GROUPS = D_MODEL // 256
B_WIDTH = B_GROUPS * HEAD_DIM
B_COLS = 2 * B_WIDTH
MEM_TOKENS = 256
MEM_HEADS = 4
MEM_WIDTH = MEM_HEADS * HEAD_DIM
MIX_OUT = A_OUT + MEM_WIDTH
D_FF = 4 * D_MODEL
N_LAYERS_A = (DEPTH + 1) // 2
N_LAYERS_B = DEPTH // 2
ALPHA = (2.0 * DEPTH) ** 0.25
BETA = (8.0 * DEPTH) ** -0.25
LN_EPS = 1e-5

kernel_name = "hybrid_dilated_gmlp_memory_deepnorm"


def _layer_norm(x, g, b):
    xf = x.astype(jnp.float32)
    mu = jnp.mean(xf, axis=-1, keepdims=True)
    var = jnp.mean(jnp.square(xf - mu), axis=-1, keepdims=True)
    y = (xf - mu) * lax.rsqrt(var + LN_EPS) * g.astype(jnp.float32) + b.astype(jnp.float32)
    return y.astype(x.dtype)


def _dilated_branch(q, k, v, window, dilation):
    b, s, h, dh = q.shape
    w_sub = window // dilation
    blk = w_sub
    L = s // dilation
    nb = -(-L // blk)
    Lp = nb * blk

    def to_sub(t):
        t = t.reshape(b, L, dilation, h, dh).transpose(0, 2, 1, 3, 4)
        t = jnp.pad(t, ((0, 0), (0, 0), (0, Lp - L), (0, 0), (0, 0)))
        return t.reshape(b, dilation, nb, blk, h, dh)

    qb, kb, vb = to_sub(q), to_sub(k), to_sub(v)

    def with_prev(t):
        prev = jnp.pad(t[:, :, :-1], ((0, 0), (0, 0), (1, 0), (0, 0), (0, 0), (0, 0)))
        return jnp.concatenate([prev, t], axis=3)

    kk, vv = with_prev(kb), with_prev(vb)
    scores = jnp.einsum('brnqhd,brnkhd->brnhqk', qb, kk).astype(jnp.float32) * (dh ** -0.5)
    qi = jnp.arange(blk)[:, None]
    kj = jnp.arange(2 * blk)[None, :]
    dist = qi + blk - kj
    band = (dist >= 0) & (dist <= w_sub)
    exists = (jnp.arange(nb) > 0)[:, None, None] | (kj >= blk)[None]
    mask = band[None] & exists
    scores = jnp.where(mask[None, None, :, None], scores, -jnp.inf)
    m = jnp.max(scores, axis=-1, keepdims=True)
    p = jnp.exp(scores - m)
    l = jnp.sum(p, axis=-1, keepdims=True)
    o = jnp.einsum('brnhqk,brnkhd->brnqhd', p, vv.astype(jnp.float32))
    o = o / jnp.moveaxis(l, 3, 4)
    lse = jnp.moveaxis((m + jnp.log(l))[..., 0], 3, 4)

    def from_sub(t):
        t = t.reshape((b, dilation, Lp) + t.shape[4:])[:, :, :L]
        t = jnp.moveaxis(t, 1, 2)
        return t.reshape((b, s) + t.shape[3:])

    return from_sub(o), from_sub(lse)


def _mixer_dilated(x, w_in):
    b, s, _ = x.shape
    proj = x @ w_in
    qkv = proj[..., :A_COLS].reshape(b, s, A_GROUPS, 3, A_HEADS, HEAD_DIM)
    mem_q = proj[..., A_COLS:]
    outs, lses = [], []
    for g, (window, dilation) in enumerate(A_PAIRS):
        o, lse = _dilated_branch(qkv[:, :, g, 0], qkv[:, :, g, 1], qkv[:, :, g, 2], window, dilation)
        outs.append(o)
        lses.append(lse)
    wts = jax.nn.softmax(jnp.stack(lses, axis=0), axis=0)
    o = jnp.sum(wts[..., None] * jnp.stack(outs, axis=0), axis=0)
    return o.reshape(b, s, A_OUT).astype(x.dtype), mem_q


def _mixer_gmlp(x, w_in, w_s, b_s, vnorm_g, vnorm_b):
    b, s, _ = x.shape
    proj = x @ w_in
    z = jax.nn.gelu(proj[..., :B_COLS], approximate=False)
    u, v = z[..., :B_WIDTH], z[..., B_WIDTH:]
    mem_q = proj[..., B_COLS:]
    v = _layer_norm(v, vnorm_g, vnorm_b)
    v = v.reshape(b, s // B_CHUNK, B_CHUNK, B_GROUPS, HEAD_DIM)
    causal = jnp.tril(jnp.ones((B_CHUNK, B_CHUNK), dtype=w_s.dtype))
    ws = w_s * causal[None]
    sg = jnp.einsum('gts,bcsgd->bctgd', ws, v) + b_s.T[:, :, None]
    return u * sg.reshape(b, s, B_WIDTH), mem_q


def _memory_attention(mem_q, mem, w_mem_kv):
    b, s, _ = mem_q.shape
    q = mem_q.reshape(b, s, MEM_HEADS, HEAD_DIM)
    kv = (mem @ w_mem_kv).reshape(b, MEM_TOKENS, 2, MEM_HEADS, HEAD_DIM)
    k, v = kv[:, :, 0], kv[:, :, 1]
    scores = jnp.einsum('bshd,bmhd->bhsm', q, k).astype(jnp.float32) * (HEAD_DIM ** -0.5)
    p = jax.nn.softmax(scores, axis=-1)
    o = jnp.einsum('bhsm,bmhd->bshd', p, v.astype(jnp.float32))
    return o.reshape(b, s, MEM_WIDTH).astype(mem_q.dtype)


def setup_inputs(seed: int = 0) -> dict:
    key = jax.random.key(seed)
    ks = jax.random.split(key, 20)

    def nrm(k, shape, std):
        return jax.random.normal(k, shape, jnp.float32) * std

    x = nrm(ks[0], (BATCH, SEQ, D_MODEL), 1.0)
    mem = nrm(ks[1], (BATCH, MEM_TOKENS, D_MODEL), 1.0)
    a_scale = np.ones((A_GROUPS, 3, A_HEADS * HEAD_DIM), np.float32)
    a_scale[:, 2, :] = BETA
    a_scale = jnp.asarray(np.concatenate([a_scale.reshape(-1), np.ones(MEM_WIDTH, np.float32)]))
    w_in_a = nrm(ks[2], (N_LAYERS_A, D_MODEL, A_COLS + MEM_WIDTH), D_MODEL ** -0.5) * a_scale
    b_scale = jnp.asarray(np.concatenate([np.full(B_COLS, BETA, np.float32), np.ones(MEM_WIDTH, np.float32)]))
    w_in_b = nrm(ks[3], (N_LAYERS_B, D_MODEL, B_COLS + MEM_WIDTH), D_MODEL ** -0.5) * b_scale
    w_s = nrm(ks[4], (N_LAYERS_B, B_GROUPS, B_CHUNK, B_CHUNK), B_CHUNK ** -0.5)
    b_s = 1.0 + nrm(ks[5], (N_LAYERS_B, B_GROUPS, B_CHUNK), 0.02)
    vnorm_g = 1.0 + nrm(ks[6], (N_LAYERS_B, B_WIDTH), 0.02)
    vnorm_b = nrm(ks[7], (N_LAYERS_B, B_WIDTH), 0.02)
    kv_scale = jnp.concatenate([jnp.ones((MEM_WIDTH,), jnp.float32), jnp.full((MEM_WIDTH,), BETA, jnp.float32)])
    w_mem_kv = nrm(ks[8], (DEPTH, D_MODEL, 2 * MEM_WIDTH), D_MODEL ** -0.5) * kv_scale
    w_out = nrm(ks[9], (DEPTH, MIX_OUT, D_MODEL), BETA * MIX_OUT ** -0.5)
    ln1_g = 1.0 + nrm(ks[10], (DEPTH, D_MODEL), 0.02)
    ln1_b = nrm(ks[11], (DEPTH, D_MODEL), 0.02)
    w_ff1 = nrm(ks[12], (DEPTH, D_MODEL, D_FF), BETA * D_MODEL ** -0.5)
    w_ff2 = nrm(ks[13], (DEPTH, D_FF, D_MODEL), BETA * D_FF ** -0.5)
    ln2_g = 1.0 + nrm(ks[14], (DEPTH, D_MODEL), 0.02)
    ln2_b = nrm(ks[15], (DEPTH, D_MODEL), 0.02)
    return {"x": x, "mem": mem, "w_in_a": w_in_a, "w_in_b": w_in_b, "w_s": w_s, "b_s": b_s,
            "vnorm_g": vnorm_g, "vnorm_b": vnorm_b, "w_mem_kv": w_mem_kv, "w_out": w_out,
            "ln1_g": ln1_g, "ln1_b": ln1_b, "w_ff1": w_ff1, "w_ff2": w_ff2,
            "ln2_g": ln2_g, "ln2_b": ln2_b}


def reference(x, mem, w_in_a, w_in_b, w_s, b_s, vnorm_g, vnorm_b, w_mem_kv, w_out,
              ln1_g, ln1_b, w_ff1, w_ff2, ln2_g, ln2_b):
    for i in range(DEPTH):
        j = i // 2
        if i % 2 == 0:
            mix, mem_q = _mixer_dilated(x, w_in_a[j])
        else:
            mix, mem_q = _mixer_gmlp(x, w_in_b[j], w_s[j], b_s[j], vnorm_g[j], vnorm_b[j])
        mem_o = _memory_attention(mem_q, mem, w_mem_kv[i])
        y = jnp.concatenate([mix, mem_o], axis=-1) @ w_out[i]
        x = _layer_norm(ALPHA * x + y, ln1_g[i], ln1_b[i])
        hidden = jnp.square(jax.nn.relu(x @ w_ff1[i]))
        x = _layer_norm(ALPHA * x + hidden @ w_ff2[i], ln2_g[i], ln2_b[i])
    return x
```

```python
import functools

import numpy as np
import jax
import jax.numpy as jnp
from jax import lax
from jax.experimental import pallas as pl
from jax.experimental.pallas import tpu as pltpu

F32 = jnp.float32
BF16 = jnp.bfloat16

HEAD_DIM = 128
A_PAIRS = ((128, 1), (512, 4), (2048, 16))
A_GROUPS = len(A_PAIRS)
B_CHUNK = 128
MEM_HEADS = 4
MEM_WIDTH = MEM_HEADS * HEAD_DIM
DEPTH = 4
ALPHA = (2.0 * DEPTH) ** 0.25
LN_EPS = 1e-5
SCALE = HEAD_DIM ** -0.5
SQRT_HALF = np.float32(np.sqrt(0.5))

LANES = 128
VMEM_LIMIT_BYTES = 56 * 1024 * 1024

PROJ_TM, PROJ_TN = 1024, 1024
ATTN_ROWS = 512
MIX_TM = 512
FFN_TM, FFN_TF = 512, 1024

_NT = (((1,), (1,)), ((), ()))


def _params(*sem):
    return pltpu.CompilerParams(dimension_semantics=sem, vmem_limit_bytes=VMEM_LIMIT_BYTES)


def _layer_norm(x, g, b):
    mu = jnp.mean(x, axis=-1, keepdims=True)
    xc = x - mu
    var = jnp.mean(xc * xc, axis=-1, keepdims=True)
    return xc * lax.rsqrt(var + LN_EPS) * g + b


def _memkv_kernel(mem_ref, w_ref, o_ref):
    o_ref[0, 0] = jnp.dot(mem_ref[0].astype(BF16), w_ref[0].astype(BF16),
                          preferred_element_type=F32).astype(BF16)


def _memkv(mem, w_mem_kv):
    nb, m, d = mem.shape
    depth, _, n = w_mem_kv.shape
    return pl.pallas_call(
        _memkv_kernel,
        grid=(depth, nb),
        in_specs=[pl.BlockSpec((1, m, d), lambda i, b: (b, 0, 0)),
                  pl.BlockSpec((1, d, n), lambda i, b: (i, 0, 0))],
        out_specs=pl.BlockSpec((1, 1, m, n), lambda i, b: (i, b, 0, 0)),
        out_shape=jax.ShapeDtypeStruct((depth, nb, m, n), BF16),
        compiler_params=_params("parallel", "arbitrary"),
        name="memkv",
    )(mem, w_mem_kv)


def _proj_kernel(x_ref, w_ref, o_ref, xb_ref, *, gelu):
    @pl.when(pl.program_id(1) == 0)
    def _():
        xb_ref[...] = x_ref[...].astype(BF16)

    acc = jnp.dot(xb_ref[...], w_ref[0], preferred_element_type=F32)
    if gelu:
        acc = 0.5 * acc * (1.0 + lax.erf(acc * SQRT_HALF))
    o_ref[...] = acc.astype(o_ref.dtype)


def _proj(x, w, layer, n_out, gelu):
    t, d = x.shape
    return pl.pallas_call(
        functools.partial(_proj_kernel, gelu=gelu),
        grid=(t // PROJ_TM, n_out // PROJ_TN),
        in_specs=[pl.BlockSpec((PROJ_TM, d), lambda i, j: (i, 0)),
                  pl.BlockSpec((1, d, PROJ_TN), lambda i, j: (layer, 0, j))],
        out_specs=pl.BlockSpec((PROJ_TM, PROJ_TN), lambda i, j: (i, j)),
        out_shape=jax.ShapeDtypeStruct((t, n_out), BF16),
        scratch_shapes=[pltpu.VMEM((PROJ_TM, d), BF16)],
        compiler_params=_params("parallel", "arbitrary"),
        name="proj_gelu" if gelu else "proj",
    )(x, w)


def _attn_kernel(q_ref, kc_ref, kp_ref, vc_ref, vp_ref, o_ref, lse_ref, kbuf, vbuf, *, rows, heads):
    blk = HEAD_DIM
    chunk = pl.program_id(2)
    kbuf[0:blk, :] = kp_ref[0]
    kbuf[blk:, :] = kc_ref[0]
    vbuf[0:blk, :] = vp_ref[0]
    vbuf[blk:, :] = vc_ref[0]

    row = lax.broadcasted_iota(jnp.int32, (blk, 2 * blk), 0)
    col = lax.broadcasted_iota(jnp.int32, (blk, 2 * blk), 1)
    band = (col >= row) & (col <= row + blk)
    lane = lax.broadcasted_iota(jnp.int32, (blk, LANES), 1)
    neg_inf = jnp.float32(-jnp.inf)
    bias_inner = jnp.where(band, 0.0, neg_inf)
    bias_first = jnp.where(band & ((col >= blk) | (chunk > 0)), 0.0, neg_inf)

    for qb in range(rows // blk):
        bias = bias_first if qb == 0 else bias_inner
        r0 = qb * blk
        lse_tile = jnp.zeros((blk, LANES), F32)
        for h in range(heads):
            c0 = h * HEAD_DIM
            q = q_ref[0, r0:r0 + blk, c0:c0 + HEAD_DIM]
            k = kbuf[r0:r0 + 2 * blk, c0:c0 + HEAD_DIM]
            v = vbuf[r0:r0 + 2 * blk, c0:c0 + HEAD_DIM]
            s = lax.dot_general(q, k, _NT, preferred_element_type=F32) * SCALE + bias
            m = jnp.max(s, axis=-1, keepdims=True)
            p = jnp.exp(s - m)
            l = jnp.sum(p, axis=-1, keepdims=True)
            o = jnp.dot(p.astype(BF16), v, preferred_element_type=F32) / l
            o_ref[0, r0:r0 + blk, c0:c0 + HEAD_DIM] = o.astype(o_ref.dtype)
            lse_tile = jnp.where(lane == h, m + jnp.log(l), lse_tile)
        lse_ref[0, r0:r0 + blk, :] = lse_tile


def _attn_group(qkv, batch, seq, group, window, dilation):
    t, cols = qkv.shape
    width = cols // (A_GROUPS * 3)
    heads = width // HEAD_DIM
    assert window // dilation == HEAD_DIM
    sub_len = seq // dilation
    rows = min(ATTN_ROWS, sub_len)
    per_row = cols // width
    qkv_v = qkv.reshape(batch, sub_len, dilation * cols)
    rblk = rows // HEAD_DIM

    def cur(which):
        return pl.BlockSpec((1, rows, width),
                            lambda b, r, c: (b, c, r * per_row + group * 3 + which))

    def prev(which):
        return pl.BlockSpec((1, HEAD_DIM, width),
                            lambda b, r, c: (b, jnp.maximum(c * rblk - 1, 0), r * per_row + group * 3 + which))

    o, lse = pl.pallas_call(
        functools.partial(_attn_kernel, rows=rows, heads=heads),
        grid=(batch, dilation, sub_len // rows),
        in_specs=[cur(0), cur(1), prev(1), cur(2), prev(2)],
        out_specs=[pl.BlockSpec((1, rows, width), lambda b, r, c: (b, c, r)),
                   pl.BlockSpec((1, rows, LANES), lambda b, r, c: (b, c, r))],
        out_shape=[jax.ShapeDtypeStruct((batch, sub_len, dilation * width), BF16),
                   jax.ShapeDtypeStruct((batch, sub_len, dilation * LANES), F32)],
        scratch_shapes=[pltpu.VMEM((rows + HEAD_DIM, width), BF16),
                        pltpu.VMEM((rows + HEAD_DIM, width), BF16)],
        compiler_params=_params("parallel", "parallel", "arbitrary"),
        name=f"attn_d{dilation}",
    )(qkv_v, qkv_v, qkv_v, qkv_v, qkv_v)
    return o.reshape(t, width), lse.reshape(t, LANES)


def _mem_attn_outproj_ln(x_ref, wmq_ref, kv_ref, wout_ref, g_ref, b_ref, out_ref, cat_ref, mix_width):
    xf = x_ref[...]
    mq = jnp.dot(xf.astype(BF16), wmq_ref[0], preferred_element_type=F32).astype(BF16)
    for h in range(MEM_HEADS):
        c0 = h * HEAD_DIM
        k = kv_ref[0, 0, :, c0:c0 + HEAD_DIM]
        v = kv_ref[0, 0, :, MEM_WIDTH + c0:MEM_WIDTH + c0 + HEAD_DIM]
        s = lax.dot_general(mq[:, c0:c0 + HEAD_DIM], k, _NT, preferred_element_type=F32) * SCALE
        p = jnp.exp(s - jnp.max(s, axis=-1, keepdims=True))
        l = jnp.sum(p, axis=-1, keepdims=True)
        o = jnp.dot(p.astype(BF16), v, preferred_element_type=F32) / l
        cat_ref[:, mix_width + c0:mix_width + c0 + HEAD_DIM] = o.astype(BF16)
    y = jnp.dot(cat_ref[...], wout_ref[0], preferred_element_type=F32)
    out_ref[...] = _layer_norm(ALPHA * xf + y, g_ref[0], b_ref[0])


def _mix_a_kernel(o0_ref, o1_ref, o2_ref, l0_ref, l1_ref, l2_ref, x_ref, wmq_ref, kv_ref, wout_ref,
                  g_ref, b_ref, out_ref, cat_ref, *, heads):
    l0, l1, l2 = l0_ref[...], l1_ref[...], l2_ref[...]
    mx = jnp.maximum(jnp.maximum(l0, l1), l2)
    e0, e1, e2 = jnp.exp(l0 - mx), jnp.exp(l1 - mx), jnp.exp(l2 - mx)
    den = e0 + e1 + e2
    w0, w1, w2 = e0 / den, e1 / den, e2 / den
    for h in range(heads):
        c0 = h * HEAD_DIM
        sl = slice(c0, c0 + HEAD_DIM)
        mix = (w0[:, h:h + 1] * o0_ref[:, sl].astype(F32)
               + w1[:, h:h + 1] * o1_ref[:, sl].astype(F32)
               + w2[:, h:h + 1] * o2_ref[:, sl].astype(F32))
        cat_ref[:, sl] = mix.astype(BF16)
    _mem_attn_outproj_ln(x_ref, wmq_ref, kv_ref, wout_ref, g_ref, b_ref, out_ref, cat_ref, heads * HEAD_DIM)


def _mix_b_kernel(u_ref, v_ref, ws_ref, bs_ref, vg_ref, vb_ref, x_ref, wmq_ref, kv_ref, wout_ref,
                  g_ref, b_ref, out_ref, cat_ref, vn_ref, *, groups):
    tm = u_ref.shape[0]
    vn_ref[...] = _layer_norm(v_ref[...].astype(F32), vg_ref[0], vb_ref[0]).astype(BF16)
    row = lax.broadcasted_iota(jnp.int32, (B_CHUNK, B_CHUNK), 0)
    col = lax.broadcasted_iota(jnp.int32, (B_CHUNK, B_CHUNK), 1)
    causal = (col <= row).astype(F32)
    for g in range(groups):
        c0 = g * HEAD_DIM
        sl = slice(c0, c0 + HEAD_DIM)
        ws = (ws_ref[0, g] * causal).astype(BF16)
        bias = bs_ref[0, :, g:g + 1]
        for c in range(tm // B_CHUNK):
            rs = slice(c * B_CHUNK, (c + 1) * B_CHUNK)
            sg = jnp.dot(ws, vn_ref[rs, sl], preferred_element_type=F32) + bias
            cat_ref[rs, sl] = (u_ref[rs, sl].astype(F32) * sg).astype(BF16)
    _mem_attn_outproj_ln(x_ref, wmq_ref, kv_ref, wout_ref, g_ref, b_ref, out_ref, cat_ref, groups * HEAD_DIM)


def _mix_tail_in_specs(d, seq, layer_in, mq_block, layer, kv_shape, wout_shape):
    tiles_per_batch = seq // MIX_TM
    _, _, m, kvw = kv_shape
    _, k_out, _ = wout_shape
    return [
        pl.BlockSpec((MIX_TM, d), lambda i: (i, 0)),
        pl.BlockSpec((1, d, MEM_WIDTH), lambda i: (layer_in, 0, mq_block)),
        pl.BlockSpec((1, 1, m, kvw), lambda i: (layer, i // tiles_per_batch, 0, 0)),
        pl.BlockSpec((1, k_out, d), lambda i: (layer, 0, 0)),
        pl.BlockSpec((1, 1, d), lambda i: (layer, 0, 0)),
        pl.BlockSpec((1, 1, d), lambda i: (layer, 0, 0)),
    ]


def _mix_a(x, outs, lses, seq, w_in, layer_in, kv, w_out, ln_g, ln_b, layer):
    t, d = x.shape
    width = outs[0].shape[1]
    mq_block = (w_in.shape[2] - MEM_WIDTH) // MEM_WIDTH
    tail = _mix_tail_in_specs(d, seq, layer_in, mq_block, layer, kv.shape, w_out.shape)
    return pl.pallas_call(
        functools.partial(_mix_a_kernel, heads=width // HEAD_DIM),
        grid=(t // MIX_TM,),
        in_specs=[pl.BlockSpec((MIX_TM, width), lambda i: (i, 0))] * 3
                 + [pl.BlockSpec((MIX_TM, LANES), lambda i: (i, 0))] * 3 + tail,
        out_specs=pl.BlockSpec((MIX_TM, d), lambda i: (i, 0)),
        out_shape=jax.ShapeDtypeStruct((t, d), F32),
        scratch_shapes=[pltpu.VMEM((MIX_TM, w_out.shape[1]), BF16)],
        compiler_params=_params("parallel"),
        name="mix_a",
    )(*outs, *lses, x, w_in, kv, w_out, ln_g, ln_b)


def _mix_b(x, z, seq, w_s, b_s_t, vn_g, vn_b, w_in, layer_in, kv, w_out, ln_g, ln_b, layer):
    t, d = x.shape
    width = z.shape[1] // 2
    groups = width // HEAD_DIM
    mq_block = (w_in.shape[2] - MEM_WIDTH) // MEM_WIDTH
    tail = _mix_tail_in_specs(d, seq, layer_in, mq_block, layer, kv.shape, w_out.shape)
    return pl.pallas_call(
        functools.partial(_mix_b_kernel, groups=groups),
        grid=(t // MIX_TM,),
        in_specs=[pl.BlockSpec((MIX_TM, width), lambda i: (i, 0)),
                  pl.BlockSpec((MIX_TM, width), lambda i: (i, 1)),
                  pl.BlockSpec((1, groups, B_CHUNK, B_CHUNK), lambda i: (layer_in, 0, 0, 0)),
                  pl.BlockSpec((1, B_CHUNK, groups), lambda i: (layer_in, 0, 0)),
                  pl.BlockSpec((1, 1, width), lambda i: (layer_in, 0, 0)),
                  pl.BlockSpec((1, 1, width), lambda i: (layer_in, 0, 0))] + tail,
        out_specs=pl.BlockSpec((MIX_TM, d), lambda i: (i, 0)),
        out_shape=jax.ShapeDtypeStruct((t, d), F32),
        scratch_shapes=[pltpu.VMEM((MIX_TM, w_out.shape[1]), BF16),
                        pltpu.VMEM((MIX_TM, width), BF16)],
        compiler_params=_params("parallel"),
        name="mix_b",
    )(z, z, w_s, b_s_t, vn_g, vn_b, x, w_in, kv, w_out, ln_g, ln_b)


def _ffn_kernel(x_ref, w1_ref, w2_ref, g_ref, b_ref, o_ref, xb_ref):
    j = pl.program_id(1)

    @pl.when(j == 0)
    def _():
        xb_ref[...] = x_ref[...].astype(BF16)

    h = jnp.maximum(jnp.dot(xb_ref[...], w1_ref[0], preferred_element_type=F32), 0.0)
    part = jnp.dot((h * h).astype(BF16), w2_ref[0], preferred_element_type=F32)

    @pl.when(j == 0)
    def _():
        o_ref[...] = part

    @pl.when(j > 0)
    def _():
        o_ref[...] += part

    @pl.when(j == pl.num_programs(1) - 1)
    def _():
        o_ref[...] = _layer_norm(ALPHA * x_ref[...] + o_ref[...], g_ref[0], b_ref[0])


def _ffn(x, w1, w2, ln_g, ln_b, layer):
    t, d = x.shape
    d_ff = w1.shape[2]
    return pl.pallas_call(
        _ffn_kernel,
        grid=(t // FFN_TM, d_ff // FFN_TF),
        in_specs=[pl.BlockSpec((FFN_TM, d), lambda i, j: (i, 0)),
                  pl.BlockSpec((1, d, FFN_TF), lambda i, j: (layer, 0, j)),
                  pl.BlockSpec((1, FFN_TF, d), lambda i, j: (layer, j, 0)),
                  pl.BlockSpec((1, 1, d), lambda i, j: (layer, 0, 0)),
                  pl.BlockSpec((1, 1, d), lambda i, j: (layer, 0, 0))],
        out_specs=pl.BlockSpec((FFN_TM, d), lambda i, j: (i, 0)),
        out_shape=jax.ShapeDtypeStruct((t, d), F32),
        scratch_shapes=[pltpu.VMEM((FFN_TM, d), BF16)],
        compiler_params=_params("parallel", "arbitrary"),
        name="ffn",
    )(x, w1, w2, ln_g, ln_b)


def kernel(x, mem, w_in_a, w_in_b, w_s, b_s, vnorm_g, vnorm_b, w_mem_kv, w_out, ln1_g, ln1_b, w_ff1, w_ff2,
           ln2_g, ln2_b):
    batch, seq, d = x.shape
    xf = x.reshape(batch * seq, d)
    wa, wb = w_in_a.astype(BF16), w_in_b.astype(BF16)
    wo, w1, w2 = w_out.astype(BF16), w_ff1.astype(BF16), w_ff2.astype(BF16)
    row3 = lambda a: a.reshape(a.shape[0], 1, a.shape[1])
    ln1_g, ln1_b, ln2_g, ln2_b = row3(ln1_g), row3(ln1_b), row3(ln2_g), row3(ln2_b)
    vnorm_g, vnorm_b = row3(vnorm_g), row3(vnorm_b)
    b_s_t = jnp.swapaxes(b_s, 1, 2)
    a_cols = wa.shape[2] - MEM_WIDTH
    b_cols = wb.shape[2] - MEM_WIDTH

    kv = _memkv(mem, w_mem_kv)
    for i in range(DEPTH):
        j = i // 2
        if i % 2 == 0:
            qkv = _proj(xf, wa, j, a_cols, gelu=False)
            outs, lses = [], []
            for g, (window, dilation) in enumerate(A_PAIRS):
                o, lse = _attn_group(qkv, batch, seq, g, window, dilation)
                outs.append(o)
                lses.append(lse)
            xf = _mix_a(xf, outs, lses, seq, wa, j, kv, wo, ln1_g, ln1_b, i)
        else:
            z = _proj(xf, wb, j, b_cols, gelu=True)
            xf = _mix_b(xf, z, seq, w_s, b_s_t, vnorm_g, vnorm_b, wb, j, kv, wo, ln1_g, ln1_b, i)
        xf = _ffn(xf, w1, w2, ln2_g, ln2_b, i)
    return xf.reshape(batch, seq, d)
```

```python
import functools

import numpy as np
import jax
import jax.numpy as jnp
from jax import lax
from jax.experimental import pallas as pl
from jax.experimental.pallas import tpu as pltpu

F32 = jnp.float32
BF16 = jnp.bfloat16

HEAD_DIM = 128
A_PAIRS = ((128, 1), (512, 4), (2048, 16))
A_GROUPS = len(A_PAIRS)
A_WINDOW = 128
B_CHUNK = 128
MEM_HEADS = 4
MEM_WIDTH = MEM_HEADS * HEAD_DIM
DEPTH = 4
ALPHA = (2.0 * DEPTH) ** 0.25
LN_EPS = 1e-5
SCALE = HEAD_DIM ** -0.5
SQRT_HALF = np.float32(np.sqrt(0.5))

LANES = 128
VMEM_LIMIT_BYTES = 56 * 1024 * 1024

RUN = 16
SUPER = RUN * RUN

PROJ_TM, PROJ_TN = 1024, 1024
ATTN_ROWS = 512
ATTN_TILE = 128
MIX_TM = 512
FFN_TM, FFN_TF = 512, 1024

_NT = (((1,), (1,)), ((), ()))


def _params(*sem):
    return pltpu.CompilerParams(dimension_semantics=sem, vmem_limit_bytes=VMEM_LIMIT_BYTES)


def _layer_norm(x, g, b):
    mu = jnp.mean(x, axis=-1, keepdims=True)
    xc = x - mu
    var = jnp.mean(xc * xc, axis=-1, keepdims=True)
    return xc * lax.rsqrt(var + LN_EPS) * g + b


def _perm_kernel(x_ref, o_ref, slab_ref):
    n_slabs = slab_ref.shape[0]
    for c in range(n_slabs):
        slab_ref[c] = x_ref[:, c * LANES:(c + 1) * LANES]
    for r in range(RUN):
        for c in range(n_slabs):
            o_ref[r * RUN:(r + 1) * RUN, c * LANES:(c + 1) * LANES] = slab_ref[c, pl.ds(r, RUN, stride=RUN), :]


def _perm(x):
    t, d = x.shape
    return pl.pallas_call(
        _perm_kernel,
        grid=(t // SUPER,),
        in_specs=[pl.BlockSpec((SUPER, d), lambda i: (i, 0))],
        out_specs=pl.BlockSpec((SUPER, d), lambda i: (i, 0)),
        out_shape=jax.ShapeDtypeStruct((t, d), x.dtype),
        scratch_shapes=[pltpu.VMEM((d // LANES, SUPER, LANES), x.dtype)],
        compiler_params=_params("parallel"),
        name="perm",
    )(x)


def _memkv_kernel(mem_ref, w_ref, o_ref):
    o_ref[0, 0] = jnp.dot(mem_ref[0].astype(BF16), w_ref[0].astype(BF16),
                          preferred_element_type=F32).astype(BF16)


def _memkv(mem, w_mem_kv):
    nb, m, d = mem.shape
    depth, _, n = w_mem_kv.shape
    return pl.pallas_call(
        _memkv_kernel,
        grid=(depth, nb),
        in_specs=[pl.BlockSpec((1, m, d), lambda i, b: (b, 0, 0)),
                  pl.BlockSpec((1, d, n), lambda i, b: (i, 0, 0))],
        out_specs=pl.BlockSpec((1, 1, m, n), lambda i, b: (i, b, 0, 0)),
        out_shape=jax.ShapeDtypeStruct((depth, nb, m, n), BF16),
        compiler_params=_params("parallel", "arbitrary"),
        name="memkv",
    )(mem, w_mem_kv)


def _proj_kernel(x_ref, w_ref, o_ref, xb_ref, *, gelu):
    @pl.when(pl.program_id(1) == 0)
    def _():
        xb_ref[...] = x_ref[...].astype(BF16)

    acc = jnp.dot(xb_ref[...], w_ref[0], preferred_element_type=F32)
    if gelu:
        acc = 0.5 * acc * (1.0 + lax.erf(acc * SQRT_HALF))
    o_ref[...] = acc.astype(o_ref.dtype)


def _proj(x, w, layer, n_out, gelu):
    t, d = x.shape
    return pl.pallas_call(
        functools.partial(_proj_kernel, gelu=gelu),
        grid=(t // PROJ_TM, n_out // PROJ_TN),
        in_specs=[pl.BlockSpec((PROJ_TM, d), lambda i, j: (i, 0)),
                  pl.BlockSpec((1, d, PROJ_TN), lambda i, j: (layer, 0, j))],
        out_specs=pl.BlockSpec((PROJ_TM, PROJ_TN), lambda i, j: (i, j)),
        out_shape=jax.ShapeDtypeStruct((t, n_out), BF16),
        scratch_shapes=[pltpu.VMEM((PROJ_TM, d), BF16)],
        compiler_params=_params("parallel", "arbitrary"),
        name="proj_gelu" if gelu else "proj",
    )(x, w)


def _class_pos(i, dilation):
    per_super = SUPER // dilation
    n_a = RUN // dilation
    return (i & ~(per_super - 1)) + n_a * (i & (RUN - 1)) + ((i >> 4) & (n_a - 1))


def _attn_kernel(q_ref, kc_ref, kp_ref, vc_ref, vp_ref, o_ref, lse_ref,
                 qbuf, kbuf, vbuf, obuf, lbuf, bias_ref, *, rows, unit, heads, dilation):
    n_keys = 2 * unit
    width = heads * HEAD_DIM
    chunk = pl.program_id(2)
    first_step = (pl.program_id(0) == 0) & (pl.program_id(1) == 0) & (chunk == 0)

    @pl.when(first_step)
    def _():
        neg_inf = jnp.float32(-jnp.inf)
        for qt in range(unit // ATTN_TILE):
            row = lax.broadcasted_iota(jnp.int32, (ATTN_TILE, n_keys), 0) + qt * ATTN_TILE
            col = lax.broadcasted_iota(jnp.int32, (ATTN_TILE, n_keys), 1)
            dist = unit + _class_pos(row, dilation) - (_class_pos(col & (unit - 1), dilation) + (col & unit))
            band = (dist >= 0) & (dist <= A_WINDOW)
            bias_ref[0, qt] = jnp.where(band & (col >= unit), 0.0, neg_inf)
            bias_ref[1, qt] = jnp.where(band, 0.0, neg_inf)

    qbuf[...] = q_ref[...].reshape(rows, width)
    kbuf[0:unit, :] = kp_ref[...].reshape(unit, width)
    kbuf[unit:, :] = kc_ref[...].reshape(rows, width)
    vbuf[0:unit, :] = vp_ref[...].reshape(unit, width)
    vbuf[unit:, :] = vc_ref[...].reshape(rows, width)

    has_prev = jnp.where(chunk > 0, 1, 0)
    lane = lax.broadcasted_iota(jnp.int32, (ATTN_TILE, LANES), 1)
    for u in range(rows // unit):
        k0 = u * unit
        for qt in range(unit // ATTN_TILE):
            bias = bias_ref[has_prev if u == 0 else 1, qt]
            r0 = k0 + qt * ATTN_TILE
            lse_tile = jnp.zeros((ATTN_TILE, LANES), F32)
            for h in range(heads):
                cs = slice(h * HEAD_DIM, (h + 1) * HEAD_DIM)
                s = lax.dot_general(qbuf[r0:r0 + ATTN_TILE, cs], kbuf[k0:k0 + n_keys, cs], _NT,
                                    preferred_element_type=F32) * SCALE + bias
                m = jnp.max(s, axis=-1, keepdims=True)
                p = jnp.exp(s - m)
                l = jnp.sum(p, axis=-1, keepdims=True)
                o = jnp.dot(p.astype(BF16), vbuf[k0:k0 + n_keys, cs], preferred_element_type=F32) / l
                obuf[r0:r0 + ATTN_TILE, cs] = o.astype(BF16)
                lse_tile = jnp.where(lane == h, m + jnp.log(l), lse_tile)
            lbuf[r0:r0 + ATTN_TILE, :] = lse_tile
    o_ref[...] = obuf[...].reshape(o_ref.shape)
    lse_ref[...] = lbuf[...].reshape(lse_ref.shape)


def _attn_group(qkv, batch, seq, group, window, dilation):
    t, cols = qkv.shape
    width = cols // (A_GROUPS * 3)
    heads = width // HEAD_DIM
    assert window // dilation == A_WINDOW and RUN % dilation == 0
    n_super = seq // SUPER
    n_a = RUN // dilation
    per_super = SUPER // dilation
    unit = max(A_WINDOW, per_super)
    rows = min(ATTN_ROWS, seq // dilation)
    assert rows % unit == 0 and unit % per_super == 0
    per_row = cols // width

    def view(a, last):
        return a.reshape(batch, n_super, n_a, dilation, RUN, last)

    def cur(last, col_block):
        return pl.BlockSpec((None, rows // per_super, n_a, None, RUN, last),
                            lambda b, r, c: (b, c, 0, r, 0, col_block))

    def prev(col_block):
        return pl.BlockSpec((None, unit // per_super, n_a, None, RUN, width),
                            lambda b, r, c: (b, jnp.maximum(c * (rows // unit) - 1, 0), 0, r, 0, col_block))

    qkv_v = view(qkv, cols)
    q_col, k_col, v_col = (group * 3 + i for i in range(3))
    o, lse = pl.pallas_call(
        functools.partial(_attn_kernel, rows=rows, unit=unit, heads=heads, dilation=dilation),
        grid=(batch, dilation, (seq // dilation) // rows),
        in_specs=[cur(width, q_col), cur(width, k_col), prev(k_col), cur(width, v_col), prev(v_col)],
        out_specs=[cur(width, 0), cur(LANES, 0)],
        out_shape=[jax.ShapeDtypeStruct((batch, n_super, n_a, dilation, RUN, width), BF16),
                   jax.ShapeDtypeStruct((batch, n_super, n_a, dilation, RUN, LANES), F32)],
        scratch_shapes=[pltpu.VMEM((rows, width), BF16),
                        pltpu.VMEM((rows + unit, width), BF16),
                        pltpu.VMEM((rows + unit, width), BF16),
                        pltpu.VMEM((rows, width), BF16),
                        pltpu.VMEM((rows, LANES), F32),
                        pltpu.VMEM((2, unit // ATTN_TILE, ATTN_TILE, 2 * unit), F32)],
        compiler_params=_params("arbitrary", "arbitrary", "arbitrary"),
        name=f"attn_d{dilation}",
    )(qkv_v, qkv_v, qkv_v, qkv_v, qkv_v)
    return o.reshape(t, width), lse.reshape(t, LANES)


def _mem_attn_outproj_ln(x_ref, wmq_ref, kv_ref, wout_ref, g_ref, b_ref, out_ref, cat_ref, mix_width):
    xf = x_ref[...]
    mq = jnp.dot(xf.astype(BF16), wmq_ref[0], preferred_element_type=F32).astype(BF16)
    for h in range(MEM_HEADS):
        c0 = h * HEAD_DIM
        k = kv_ref[0, 0, :, c0:c0 + HEAD_DIM]
        v = kv_ref[0, 0, :, MEM_WIDTH + c0:MEM_WIDTH + c0 + HEAD_DIM]
        s = lax.dot_general(mq[:, c0:c0 + HEAD_DIM], k, _NT, preferred_element_type=F32) * SCALE
        p = jnp.exp(s - jnp.max(s, axis=-1, keepdims=True))
        l = jnp.sum(p, axis=-1, keepdims=True)
        o = jnp.dot(p.astype(BF16), v, preferred_element_type=F32) / l
        cat_ref[:, mix_width + c0:mix_width + c0 + HEAD_DIM] = o.astype(BF16)
    y = jnp.dot(cat_ref[...], wout_ref[0], preferred_element_type=F32)
    out_ref[...] = _layer_norm(ALPHA * xf + y, g_ref[0], b_ref[0])


def _mix_a_kernel(o0_ref, o1_ref, o2_ref, l0_ref, l1_ref, l2_ref, x_ref, wmq_ref, kv_ref, wout_ref,
                  g_ref, b_ref, out_ref, cat_ref, *, heads):
    l0, l1, l2 = l0_ref[...], l1_ref[...], l2_ref[...]
    mx = jnp.maximum(jnp.maximum(l0, l1), l2)
    e0, e1, e2 = jnp.exp(l0 - mx), jnp.exp(l1 - mx), jnp.exp(l2 - mx)
    den = e0 + e1 + e2
    w0, w1, w2 = e0 / den, e1 / den, e2 / den
    for h in range(heads):
        c0 = h * HEAD_DIM
        sl = slice(c0, c0 + HEAD_DIM)
        mix = (w0[:, h:h + 1] * o0_ref[:, sl].astype(F32)
               + w1[:, h:h + 1] * o1_ref[:, sl].astype(F32)
               + w2[:, h:h + 1] * o2_ref[:, sl].astype(F32))
        cat_ref[:, sl] = mix.astype(BF16)
    _mem_attn_outproj_ln(x_ref, wmq_ref, kv_ref, wout_ref, g_ref, b_ref, out_ref, cat_ref, heads * HEAD_DIM)


def _mix_b_kernel(u_ref, v_ref, ws_ref, bs_ref, vg_ref, vb_ref, x_ref, wmq_ref, kv_ref, wout_ref,
                  g_ref, b_ref, out_ref, cat_ref, vn_ref, *, groups):
    tm = u_ref.shape[0]
    vn_ref[...] = _layer_norm(v_ref[...].astype(F32), vg_ref[0], vb_ref[0]).astype(BF16)
    row = lax.broadcasted_iota(jnp.int32, (SUPER, SUPER), 0)
    col = lax.broadcasted_iota(jnp.int32, (SUPER, SUPER), 1)
    nat_row = RUN * (row & (RUN - 1)) + (row >> 4)
    nat_col = RUN * (col & (RUN - 1)) + (col >> 4)
    same_chunk = (nat_row // B_CHUNK) == (nat_col // B_CHUNK)
    causal = (same_chunk & (nat_col <= nat_row)).astype(F32)
    for g in range(groups):
        sl = slice(g * HEAD_DIM, (g + 1) * HEAD_DIM)
        ws = (ws_ref[0, g] * causal).astype(BF16)
        bias = bs_ref[0, :, g:g + 1]
        for c in range(tm // SUPER):
            rs = slice(c * SUPER, (c + 1) * SUPER)
            sg = jnp.dot(ws, vn_ref[rs, sl], preferred_element_type=F32) + bias
            cat_ref[rs, sl] = (u_ref[rs, sl].astype(F32) * sg).astype(BF16)
    _mem_attn_outproj_ln(x_ref, wmq_ref, kv_ref, wout_ref, g_ref, b_ref, out_ref, cat_ref, groups * HEAD_DIM)


def _mix_tail_in_specs(d, seq, layer_in, mq_block, layer, kv_shape, wout_shape):
    tiles_per_batch = seq // MIX_TM
    _, _, m, kvw = kv_shape
    _, k_out, _ = wout_shape
    return [
        pl.BlockSpec((MIX_TM, d), lambda i: (i, 0)),
        pl.BlockSpec((1, d, MEM_WIDTH), lambda i: (layer_in, 0, mq_block)),
        pl.BlockSpec((1, 1, m, kvw), lambda i: (layer, i // tiles_per_batch, 0, 0)),
        pl.BlockSpec((1, k_out, d), lambda i: (layer, 0, 0)),
        pl.BlockSpec((1, 1, d), lambda i: (layer, 0, 0)),
        pl.BlockSpec((1, 1, d), lambda i: (layer, 0, 0)),
    ]


def _mix_a(x, outs, lses, seq, w_in, layer_in, kv, w_out, ln_g, ln_b, layer):
    t, d = x.shape
    width = outs[0].shape[1]
    mq_block = (w_in.shape[2] - MEM_WIDTH) // MEM_WIDTH
    tail = _mix_tail_in_specs(d, seq, layer_in, mq_block, layer, kv.shape, w_out.shape)
    return pl.pallas_call(
        functools.partial(_mix_a_kernel, heads=width // HEAD_DIM),
        grid=(t // MIX_TM,),
        in_specs=[pl.BlockSpec((MIX_TM, width), lambda i: (i, 0))] * 3
                 + [pl.BlockSpec((MIX_TM, LANES), lambda i: (i, 0))] * 3 + tail,
        out_specs=pl.BlockSpec((MIX_TM, d), lambda i: (i, 0)),
        out_shape=jax.ShapeDtypeStruct((t, d), F32),
        scratch_shapes=[pltpu.VMEM((MIX_TM, w_out.shape[1]), BF16)],
        compiler_params=_params("parallel"),
        name="mix_a",
    )(*outs, *lses, x, w_in, kv, w_out, ln_g, ln_b)


def _mix_b(x, z, seq, ws_perm, bs_perm, vn_g, vn_b, w_in, layer_in, kv, w_out, ln_g, ln_b, layer):
    t, d = x.shape
    width = z.shape[1] // 2
    groups = width // HEAD_DIM
    mq_block = (w_in.shape[2] - MEM_WIDTH) // MEM_WIDTH
    tail = _mix_tail_in_specs(d, seq, layer_in, mq_block, layer, kv.shape, w_out.shape)
    return pl.pallas_call(
        functools.partial(_mix_b_kernel, groups=groups),
        grid=(t // MIX_TM,),
        in_specs=[pl.BlockSpec((MIX_TM, width), lambda i: (i, 0)),
                  pl.BlockSpec((MIX_TM, width), lambda i: (i, 1)),
                  pl.BlockSpec((1, groups, SUPER, SUPER), lambda i: (layer_in, 0, 0, 0)),
                  pl.BlockSpec((1, SUPER, groups), lambda i: (layer_in, 0, 0)),
                  pl.BlockSpec((1, 1, width), lambda i: (layer_in, 0, 0)),
                  pl.BlockSpec((1, 1, width), lambda i: (layer_in, 0, 0))] + tail,
        out_specs=pl.BlockSpec((MIX_TM, d), lambda i: (i, 0)),
        out_shape=jax.ShapeDtypeStruct((t, d), F32),
        scratch_shapes=[pltpu.VMEM((MIX_TM, w_out.shape[1]), BF16),
                        pltpu.VMEM((MIX_TM, width), BF16)],
        compiler_params=_params("parallel"),
        name="mix_b",
    )(z, z, ws_perm, bs_perm, vn_g, vn_b, x, w_in, kv, w_out, ln_g, ln_b)


def _ffn_kernel(x_ref, w1_ref, w2_ref, g_ref, b_ref, o_ref, xb_ref):
    j = pl.program_id(1)

    @pl.when(j == 0)
    def _():
        xb_ref[...] = x_ref[...].astype(BF16)

    h = jnp.maximum(jnp.dot(xb_ref[...], w1_ref[0], preferred_element_type=F32), 0.0)
    part = jnp.dot((h * h).astype(BF16), w2_ref[0], preferred_element_type=F32)

    @pl.when(j == 0)
    def _():
        o_ref[...] = part

    @pl.when(j > 0)
    def _():
        o_ref[...] += part

    @pl.when(j == pl.num_programs(1) - 1)
    def _():
        o_ref[...] = _layer_norm(ALPHA * x_ref[...] + o_ref[...], g_ref[0], b_ref[0])


def _ffn(x, w1, w2, ln_g, ln_b, layer):
    t, d = x.shape
    d_ff = w1.shape[2]
    return pl.pallas_call(
        _ffn_kernel,
        grid=(t // FFN_TM, d_ff // FFN_TF),
        in_specs=[pl.BlockSpec((FFN_TM, d), lambda i, j: (i, 0)),
                  pl.BlockSpec((1, d, FFN_TF), lambda i, j: (layer, 0, j)),
                  pl.BlockSpec((1, FFN_TF, d), lambda i, j: (layer, j, 0)),
                  pl.BlockSpec((1, 1, d), lambda i, j: (layer, 0, 0)),
                  pl.BlockSpec((1, 1, d), lambda i, j: (layer, 0, 0))],
        out_specs=pl.BlockSpec((FFN_TM, d), lambda i, j: (i, 0)),
        out_shape=jax.ShapeDtypeStruct((t, d), F32),
        scratch_shapes=[pltpu.VMEM((FFN_TM, d), BF16)],
        compiler_params=_params("parallel", "arbitrary"),
        name="ffn",
    )(x, w1, w2, ln_g, ln_b)


def kernel(x, mem, w_in_a, w_in_b, w_s, b_s, vnorm_g, vnorm_b, w_mem_kv, w_out, ln1_g, ln1_b, w_ff1, w_ff2,
           ln2_g, ln2_b):
    batch, seq, d = x.shape
    wa, wb = w_in_a.astype(BF16), w_in_b.astype(BF16)
    wo, w1, w2 = w_out.astype(BF16), w_ff1.astype(BF16), w_ff2.astype(BF16)
    row3 = lambda a: a.reshape(a.shape[0], 1, a.shape[1])
    ln1_g, ln1_b, ln2_g, ln2_b = row3(ln1_g), row3(ln1_b), row3(ln2_g), row3(ln2_b)
    vnorm_g, vnorm_b = row3(vnorm_g), row3(vnorm_b)
    i = np.arange(SUPER)
    pos = (RUN * (i % RUN) + i // RUN) % B_CHUNK
    ws_perm = w_s[:, :, pos[:, None], pos[None, :]]
    bs_perm = jnp.swapaxes(b_s[:, :, pos], 1, 2)
    a_cols = wa.shape[2] - MEM_WIDTH
    b_cols = wb.shape[2] - MEM_WIDTH

    xf = _perm(x.reshape(batch * seq, d))
    kv = _memkv(mem, w_mem_kv)
    for i in range(DEPTH):
        j = i // 2
        if i % 2 == 0:
            qkv = _proj(xf, wa, j, a_cols, gelu=False)
            outs, lses = [], []
            for g, (window, dilation) in enumerate(A_PAIRS):
                o, lse = _attn_group(qkv, batch, seq, g, window, dilation)
                outs.append(o)
                lses.append(lse)
            xf = _mix_a(xf, outs, lses, seq, wa, j, kv, wo, ln1_g, ln1_b, i)
        else:
            z = _proj(xf, wb, j, b_cols, gelu=True)
            xf = _mix_b(xf, z, seq, ws_perm, bs_perm, vnorm_g, vnorm_b, wb, j, kv, wo, ln1_g, ln1_b, i)
        xf = _ffn(xf, w1, w2, ln2_g, ln2_b, i)
    return _perm(xf).reshape(batch, seq, d)
```

```python
import functools

import numpy as np
import jax
import jax.numpy as jnp
from jax import lax
from jax.experimental import pallas as pl
from jax.experimental.pallas import tpu as pltpu

F32 = jnp.float32
BF16 = jnp.bfloat16

HEAD_DIM = 128
A_PAIRS = ((128, 1), (512, 4), (2048, 16))
A_GROUPS = len(A_PAIRS)
A_WINDOW = 128
B_CHUNK = 128
MEM_HEADS = 4
MEM_WIDTH = MEM_HEADS * HEAD_DIM
DEPTH = 4
ALPHA = (2.0 * DEPTH) ** 0.25
LN_EPS = 1e-5
SCALE = HEAD_DIM ** -0.5
SQRT_HALF = np.float32(np.sqrt(0.5))

LANES = 128
VMEM_LIMIT_BYTES = 56 * 1024 * 1024

RUN = 16
SUPER = RUN * RUN

PROJ_TM, PROJ_TN = 1024, 1024
ATTN_ROWS = 512
MIX_TM = 512
FFN_TM, FFN_TF = 512, 1024

_NT = (((1,), (1,)), ((), ()))


def _params(*sem):
    return pltpu.CompilerParams(dimension_semantics=sem, vmem_limit_bytes=VMEM_LIMIT_BYTES)


def _cast_specs(jobs, step_of):
    in_specs, out_specs, out_shapes = [], [], []
    for src, layer, n_blocks in jobs:
        _, rows, cols = src.shape
        assert rows % n_blocks == 0 and (rows // n_blocks) % RUN == 0
        blk = rows // n_blocks
        in_specs.append(pl.BlockSpec(
            (1, blk, cols), lambda *ids, _l=layer, _n=n_blocks: (_l, jnp.minimum(step_of(*ids), _n - 1), 0)))
        out_specs.append(pl.BlockSpec(
            (blk, cols), lambda *ids, _n=n_blocks: (jnp.minimum(step_of(*ids), _n - 1), 0)))
        out_shapes.append(jax.ShapeDtypeStruct((rows, cols), BF16))
    return in_specs, out_specs, out_shapes


def _with_casts(body, n_in, n_out, n_cast):
    def kernel(*refs):
        ins, refs = refs[:n_in], refs[n_in:]
        cast_src, refs = refs[:n_cast], refs[n_cast:]
        outs, refs = refs[:n_out], refs[n_out:]
        cast_dst, scratch = refs[:n_cast], refs[n_cast:]
        for src, dst in zip(cast_src, cast_dst):
            dst[...] = src[0].astype(BF16)
        body(*ins, *outs, *scratch)
    return kernel


def _layer_norm(x, g, b):
    mu = jnp.mean(x, axis=-1, keepdims=True)
    xc = x - mu
    var = jnp.mean(xc * xc, axis=-1, keepdims=True)
    return xc * lax.rsqrt(var + LN_EPS) * g + b


def _perm_kernel(x_ref, o_ref, slab_ref):
    n_slabs = slab_ref.shape[0]
    for c in range(n_slabs):
        slab_ref[c] = x_ref[:, c * LANES:(c + 1) * LANES]
    for r in range(RUN):
        for c in range(n_slabs):
            o_ref[r * RUN:(r + 1) * RUN, c * LANES:(c + 1) * LANES] = slab_ref[c, pl.ds(r, RUN, stride=RUN), :]


def _perm(x):
    t, d = x.shape
    return pl.pallas_call(
        _perm_kernel,
        grid=(t // SUPER,),
        in_specs=[pl.BlockSpec((SUPER, d), lambda i: (i, 0))],
        out_specs=pl.BlockSpec((SUPER, d), lambda i: (i, 0)),
        out_shape=jax.ShapeDtypeStruct((t, d), x.dtype),
        scratch_shapes=[pltpu.VMEM((d // LANES, SUPER, LANES), x.dtype)],
        compiler_params=_params("parallel"),
        name="perm",
    )(x)


def _memkv_kernel(mem_ref, w_ref, o_ref):
    o_ref[0, 0] = jnp.dot(mem_ref[0].astype(BF16), w_ref[0].astype(BF16),
                          preferred_element_type=F32).astype(BF16)


def _memkv(mem, w_mem_kv):
    nb, m, d = mem.shape
    depth, _, n = w_mem_kv.shape
    return pl.pallas_call(
        _memkv_kernel,
        grid=(depth, nb),
        in_specs=[pl.BlockSpec((1, m, d), lambda i, b: (b, 0, 0)),
                  pl.BlockSpec((1, d, n), lambda i, b: (i, 0, 0))],
        out_specs=pl.BlockSpec((1, 1, m, n), lambda i, b: (i, b, 0, 0)),
        out_shape=jax.ShapeDtypeStruct((depth, nb, m, n), BF16),
        compiler_params=_params("parallel", "arbitrary"),
        name="memkv",
    )(mem, w_mem_kv)


def _proj_kernel(x_ref, w_ref, o_ref, xb_ref, *, gelu):
    @pl.when(pl.program_id(1) == 0)
    def _():
        xb_ref[...] = x_ref[...].astype(BF16)

    acc = jnp.dot(xb_ref[...], w_ref[...], preferred_element_type=F32)
    if gelu:
        acc = 0.5 * acc * (1.0 + lax.erf(acc * SQRT_HALF))
    o_ref[...] = acc.astype(o_ref.dtype)


def _proj(x, w, n_out, gelu, cast_jobs=()):
    t, d = x.shape
    n_j = n_out // PROJ_TN
    c_in, c_out, c_shape = _cast_specs(cast_jobs, lambda i, j: i * n_j + j)
    out, *casts = pl.pallas_call(
        _with_casts(functools.partial(_proj_kernel, gelu=gelu), 2, 1, len(cast_jobs)),
        grid=(t // PROJ_TM, n_j),
        in_specs=[pl.BlockSpec((PROJ_TM, d), lambda i, j: (i, 0)),
                  pl.BlockSpec((d, PROJ_TN), lambda i, j: (0, j))] + c_in,
        out_specs=[pl.BlockSpec((PROJ_TM, PROJ_TN), lambda i, j: (i, j))] + c_out,
        out_shape=[jax.ShapeDtypeStruct((t, n_out), BF16)] + c_shape,
        scratch_shapes=[pltpu.VMEM((PROJ_TM, d), BF16)],
        compiler_params=_params("arbitrary", "arbitrary"),
        name="proj_gelu" if gelu else "proj",
    )(x, w, *[job[0] for job in cast_jobs])
    return out, casts


def _class_pos(i, dilation):
    per_super = SUPER // dilation
    n_a = RUN // dilation
    return (i & ~(per_super - 1)) + n_a * (i & (RUN - 1)) + ((i >> 4) & (n_a - 1))


def _attn_kernel(q_ref, kc_ref, kp_ref, vc_ref, vp_ref, o_ref, lse_ref,
                 qbuf, kbuf, vbuf, obuf, lbuf, bias_ref, *, rows, unit, tile, heads, dilation):
    n_keys = 2 * unit
    width = heads * HEAD_DIM
    chunk = pl.program_id(2)
    first_step = (pl.program_id(0) == 0) & (pl.program_id(1) == 0) & (chunk == 0)

    @pl.when(first_step)
    def _():
        neg_inf = jnp.float32(-jnp.inf)
        for qt in range(unit // tile):
            row = lax.broadcasted_iota(jnp.int32, (tile, n_keys), 0) + qt * tile
            col = lax.broadcasted_iota(jnp.int32, (tile, n_keys), 1)
            dist = unit + _class_pos(row, dilation) - (_class_pos(col & (unit - 1), dilation) + (col & unit))
            band = (dist >= 0) & (dist <= A_WINDOW)
            bias_ref[0, qt] = jnp.where(band & (col >= unit), 0.0, neg_inf)
            bias_ref[1, qt] = jnp.where(band, 0.0, neg_inf)

    qbuf[...] = q_ref[...].reshape(rows, width)
    kbuf[0:unit, :] = kp_ref[...].reshape(unit, width)
    kbuf[unit:, :] = kc_ref[...].reshape(rows, width)
    vbuf[0:unit, :] = vp_ref[...].reshape(unit, width)
    vbuf[unit:, :] = vc_ref[...].reshape(rows, width)

    has_prev = jnp.where(chunk > 0, 1, 0)
    lane = lax.broadcasted_iota(jnp.int32, (tile, LANES), 1)
    for u in range(rows // unit):
        k0 = u * unit
        for qt in range(unit // tile):
            bias = bias_ref[has_prev if u == 0 else 1, qt]
            r0 = k0 + qt * tile
            lse_tile = jnp.zeros((tile, LANES), F32)
            for h in range(heads):
                cs = slice(h * HEAD_DIM, (h + 1) * HEAD_DIM)
                s = lax.dot_general(qbuf[r0:r0 + tile, cs], kbuf[k0:k0 + n_keys, cs], _NT,
                                    preferred_element_type=F32) * SCALE + bias
                m = jnp.max(s, axis=-1, keepdims=True)
                p = jnp.exp(s - m)
                l = jnp.sum(p, axis=-1, keepdims=True)
                o = jnp.dot(p.astype(BF16), vbuf[k0:k0 + n_keys, cs], preferred_element_type=F32) / l
                obuf[r0:r0 + tile, cs] = o.astype(BF16)
                lse_tile = jnp.where(lane == h, m + jnp.log(l), lse_tile)
            lbuf[r0:r0 + tile, :] = lse_tile
    o_ref[...] = obuf[...].reshape(o_ref.shape)
    lse_ref[...] = lbuf[...].reshape(lse_ref.shape)


def _attn_group(qkv, batch, seq, group, window, dilation):
    t, cols = qkv.shape
    width = cols // (A_GROUPS * 3)
    heads = width // HEAD_DIM
    assert window // dilation == A_WINDOW and RUN % dilation == 0
    n_super = seq // SUPER
    n_a = RUN // dilation
    per_super = SUPER // dilation
    unit = max(A_WINDOW, per_super)
    rows = min(ATTN_ROWS, seq // dilation)
    assert rows % unit == 0 and unit % per_super == 0
    tile = unit

    def view(a, last):
        return a.reshape(batch, n_super, n_a, dilation, RUN, last)

    def cur(last, col_block):
        return pl.BlockSpec((None, rows // per_super, n_a, None, RUN, last),
                            lambda b, r, c: (b, c, 0, r, 0, col_block))

    def prev(col_block):
        return pl.BlockSpec((None, unit // per_super, n_a, None, RUN, width),
                            lambda b, r, c: (b, jnp.maximum(c * (rows // unit) - 1, 0), 0, r, 0, col_block))

    qkv_v = view(qkv, cols)
    q_col, k_col, v_col = (group * 3 + i for i in range(3))
    o, lse = pl.pallas_call(
        functools.partial(_attn_kernel, rows=rows, unit=unit, tile=tile, heads=heads, dilation=dilation),
        grid=(batch, dilation, (seq // dilation) // rows),
        in_specs=[cur(width, q_col), cur(width, k_col), prev(k_col), cur(width, v_col), prev(v_col)],
        out_specs=[cur(width, 0), cur(LANES, 0)],
        out_shape=[jax.ShapeDtypeStruct((batch, n_super, n_a, dilation, RUN, width), BF16),
                   jax.ShapeDtypeStruct((batch, n_super, n_a, dilation, RUN, LANES), F32)],
        scratch_shapes=[pltpu.VMEM((rows, width), BF16),
                        pltpu.VMEM((rows + unit, width), BF16),
                        pltpu.VMEM((rows + unit, width), BF16),
                        pltpu.VMEM((rows, width), BF16),
                        pltpu.VMEM((rows, LANES), F32),
                        pltpu.VMEM((2, unit // tile, tile, 2 * unit), F32)],
        compiler_params=_params("arbitrary", "arbitrary", "arbitrary"),
        name=f"attn_d{dilation}",
    )(qkv_v, qkv_v, qkv_v, qkv_v, qkv_v)
    return o.reshape(t, width), lse.reshape(t, LANES)


def _mem_attn_outproj_ln(x_ref, wmq_ref, kv_ref, wout_ref, g_ref, b_ref, out_ref, cat_ref, mix_width):
    xf = x_ref[...]
    mq = jnp.dot(xf.astype(BF16), wmq_ref[...], preferred_element_type=F32).astype(BF16)
    for h in range(MEM_HEADS):
        c0 = h * HEAD_DIM
        k = kv_ref[0, 0, :, c0:c0 + HEAD_DIM]
        v = kv_ref[0, 0, :, MEM_WIDTH + c0:MEM_WIDTH + c0 + HEAD_DIM]
        s = lax.dot_general(mq[:, c0:c0 + HEAD_DIM], k, _NT, preferred_element_type=F32) * SCALE
        p = jnp.exp(s - jnp.max(s, axis=-1, keepdims=True))
        l = jnp.sum(p, axis=-1, keepdims=True)
        o = jnp.dot(p.astype(BF16), v, preferred_element_type=F32) / l
        cat_ref[:, mix_width + c0:mix_width + c0 + HEAD_DIM] = o.astype(BF16)
    y = jnp.dot(cat_ref[...], wout_ref[...], preferred_element_type=F32)
    out_ref[...] = _layer_norm(ALPHA * xf + y, g_ref[0], b_ref[0])


def _mix_a_kernel(o0_ref, o1_ref, o2_ref, l0_ref, l1_ref, l2_ref, x_ref, wmq_ref, kv_ref, wout_ref,
                  g_ref, b_ref, out_ref, cat_ref, *, heads):
    l0, l1, l2 = l0_ref[...], l1_ref[...], l2_ref[...]
    mx = jnp.maximum(jnp.maximum(l0, l1), l2)
    e0, e1, e2 = jnp.exp(l0 - mx), jnp.exp(l1 - mx), jnp.exp(l2 - mx)
    den = e0 + e1 + e2
    w0, w1, w2 = e0 / den, e1 / den, e2 / den
    for h in range(heads):
        c0 = h * HEAD_DIM
        sl = slice(c0, c0 + HEAD_DIM)
        mix = (w0[:, h:h + 1] * o0_ref[:, sl].astype(F32)
               + w1[:, h:h + 1] * o1_ref[:, sl].astype(F32)
               + w2[:, h:h + 1] * o2_ref[:, sl].astype(F32))
        cat_ref[:, sl] = mix.astype(BF16)
    _mem_attn_outproj_ln(x_ref, wmq_ref, kv_ref, wout_ref, g_ref, b_ref, out_ref, cat_ref, heads * HEAD_DIM)


def _mix_b_kernel(u_ref, v_ref, ws_ref, bs_ref, vg_ref, vb_ref, x_ref, wmq_ref, kv_ref, wout_ref,
                  g_ref, b_ref, out_ref, cat_ref, vn_ref, *, groups):
    tm = u_ref.shape[0]
    vn_ref[...] = _layer_norm(v_ref[...].astype(F32), vg_ref[0], vb_ref[0]).astype(BF16)
    row = lax.broadcasted_iota(jnp.int32, (SUPER, SUPER), 0)
    col = lax.broadcasted_iota(jnp.int32, (SUPER, SUPER), 1)
    nat_row = RUN * (row & (RUN - 1)) + (row >> 4)
    nat_col = RUN * (col & (RUN - 1)) + (col >> 4)
    same_chunk = (nat_row // B_CHUNK) == (nat_col // B_CHUNK)
    causal = (same_chunk & (nat_col <= nat_row)).astype(F32)
    for g in range(groups):
        sl = slice(g * HEAD_DIM, (g + 1) * HEAD_DIM)
        ws = (ws_ref[0, g] * causal).astype(BF16)
        bias = bs_ref[0, :, g:g + 1]
        for c in range(tm // SUPER):
            rs = slice(c * SUPER, (c + 1) * SUPER)
            sg = jnp.dot(ws, vn_ref[rs, sl], preferred_element_type=F32) + bias
            cat_ref[rs, sl] = (u_ref[rs, sl].astype(F32) * sg).astype(BF16)
    _mem_attn_outproj_ln(x_ref, wmq_ref, kv_ref, wout_ref, g_ref, b_ref, out_ref, cat_ref, groups * HEAD_DIM)


def _mix_tail_in_specs(d, seq, w_in_shape, layer, kv_shape, wout_shape):
    tiles_per_batch = seq // MIX_TM
    _, _, m, kvw = kv_shape
    k_out, _ = wout_shape
    mq_block = (w_in_shape[1] - MEM_WIDTH) // MEM_WIDTH
    return [
        pl.BlockSpec((MIX_TM, d), lambda i: (i, 0)),
        pl.BlockSpec((d, MEM_WIDTH), lambda i: (0, mq_block)),
        pl.BlockSpec((1, 1, m, kvw), lambda i: (layer, i // tiles_per_batch, 0, 0)),
        pl.BlockSpec((k_out, d), lambda i: (0, 0)),
        pl.BlockSpec((1, 1, d), lambda i: (layer, 0, 0)),
        pl.BlockSpec((1, 1, d), lambda i: (layer, 0, 0)),
    ]


def _mix_a(x, outs, lses, seq, w_in, kv, w_out, ln_g, ln_b, layer):
    t, d = x.shape
    width = outs[0].shape[1]
    tail = _mix_tail_in_specs(d, seq, w_in.shape, layer, kv.shape, w_out.shape)
    return pl.pallas_call(
        functools.partial(_mix_a_kernel, heads=width // HEAD_DIM),
        grid=(t // MIX_TM,),
        in_specs=[pl.BlockSpec((MIX_TM, width), lambda i: (i, 0))] * 3
                 + [pl.BlockSpec((MIX_TM, LANES), lambda i: (i, 0))] * 3 + tail,
        out_specs=pl.BlockSpec((MIX_TM, d), lambda i: (i, 0)),
        out_shape=jax.ShapeDtypeStruct((t, d), F32),
        scratch_shapes=[pltpu.VMEM((MIX_TM, w_out.shape[0]), BF16)],
        compiler_params=_params("parallel"),
        name="mix_a",
    )(*outs, *lses, x, w_in, kv, w_out, ln_g, ln_b)


def _mix_b(x, z, seq, ws_perm, bs_perm, vn_g, vn_b, layer_in, w_in, kv, w_out, ln_g, ln_b, layer):
    t, d = x.shape
    width = z.shape[1] // 2
    groups = width // HEAD_DIM
    tail = _mix_tail_in_specs(d, seq, w_in.shape, layer, kv.shape, w_out.shape)
    return pl.pallas_call(
        functools.partial(_mix_b_kernel, groups=groups),
        grid=(t // MIX_TM,),
        in_specs=[pl.BlockSpec((MIX_TM, width), lambda i: (i, 0)),
                  pl.BlockSpec((MIX_TM, width), lambda i: (i, 1)),
                  pl.BlockSpec((1, groups, SUPER, SUPER), lambda i: (layer_in, 0, 0, 0)),
                  pl.BlockSpec((1, SUPER, groups), lambda i: (layer_in, 0, 0)),
                  pl.BlockSpec((1, 1, width), lambda i: (layer_in, 0, 0)),
                  pl.BlockSpec((1, 1, width), lambda i: (layer_in, 0, 0))] + tail,
        out_specs=pl.BlockSpec((MIX_TM, d), lambda i: (i, 0)),
        out_shape=jax.ShapeDtypeStruct((t, d), F32),
        scratch_shapes=[pltpu.VMEM((MIX_TM, w_out.shape[0]), BF16),
                        pltpu.VMEM((MIX_TM, width), BF16)],
        compiler_params=_params("parallel"),
        name="mix_b",
    )(z, z, ws_perm, bs_perm, vn_g, vn_b, x, w_in, kv, w_out, ln_g, ln_b)


def _ffn_kernel(x_ref, w1_ref, w2_ref, g_ref, b_ref, o_ref, xb_ref):
    j = pl.program_id(1)

    @pl.when(j == 0)
    def _():
        xb_ref[...] = x_ref[...].astype(BF16)
        o_ref[...] = jnp.zeros_like(o_ref)

    h = jnp.maximum(jnp.dot(xb_ref[...], w1_ref[...], preferred_element_type=F32), 0.0)
    o_ref[...] += jnp.dot((h * h).astype(BF16), w2_ref[...], preferred_element_type=F32)

    @pl.when(j == pl.num_programs(1) - 1)
    def _():
        o_ref[...] = _layer_norm(ALPHA * x_ref[...] + o_ref[...], g_ref[0], b_ref[0])


def _ffn(x, w1, w2, ln_g, ln_b, layer, cast_jobs=()):
    t, d = x.shape
    d_ff = w1.shape[1]
    n_j = d_ff // FFN_TF
    c_in, c_out, c_shape = _cast_specs(cast_jobs, lambda i, j: i * n_j + j)
    out, *casts = pl.pallas_call(
        _with_casts(_ffn_kernel, 5, 1, len(cast_jobs)),
        grid=(t // FFN_TM, n_j),
        in_specs=[pl.BlockSpec((FFN_TM, d), lambda i, j: (i, 0)),
                  pl.BlockSpec((d, FFN_TF), lambda i, j: (0, j)),
                  pl.BlockSpec((FFN_TF, d), lambda i, j: (j, 0)),
                  pl.BlockSpec((1, 1, d), lambda i, j: (layer, 0, 0)),
                  pl.BlockSpec((1, 1, d), lambda i, j: (layer, 0, 0))] + c_in,
        out_specs=[pl.BlockSpec((FFN_TM, d), lambda i, j: (i, 0))] + c_out,
        out_shape=[jax.ShapeDtypeStruct((t, d), F32)] + c_shape,
        scratch_shapes=[pltpu.VMEM((FFN_TM, d), BF16)],
        compiler_params=_params("arbitrary", "arbitrary"),
        name="ffn",
    )(x, w1, w2, ln_g, ln_b, *[job[0] for job in cast_jobs])
    return out, casts


def _gmlp_params_in_working_order(w_s, b_s):
    n_layers, groups, chunk, _ = w_s.shape
    half = chunk // RUN
    reps = SUPER // chunk
    w = w_s.reshape(n_layers, groups, half, RUN, half, RUN).transpose(0, 1, 3, 2, 5, 4)
    w = jnp.broadcast_to(w[:, :, :, None, :, :, None, :], (n_layers, groups, RUN, reps, half, RUN, reps, half))
    b = b_s.reshape(n_layers, groups, half, RUN).transpose(0, 3, 2, 1)
    b = jnp.broadcast_to(b[:, :, None], (n_layers, RUN, reps, half, groups))
    return w.reshape(n_layers, groups, SUPER, SUPER), b.reshape(n_layers, SUPER, groups)


def kernel(x, mem, w_in_a, w_in_b, w_s, b_s, vnorm_g, vnorm_b, w_mem_kv, w_out, ln1_g, ln1_b, w_ff1, w_ff2,
           ln2_g, ln2_b):
    batch, seq, d = x.shape
    row3 = lambda a: a.reshape(a.shape[0], 1, a.shape[1])
    ln1_g, ln1_b, ln2_g, ln2_b = row3(ln1_g), row3(ln1_b), row3(ln2_g), row3(ln2_b)
    vnorm_g, vnorm_b = row3(vnorm_g), row3(vnorm_b)
    ws_perm, bs_perm = _gmlp_params_in_working_order(w_s, b_s)
    a_cols = w_in_a.shape[2] - MEM_WIDTH
    b_cols = w_in_b.shape[2] - MEM_WIDTH

    def w_in_of(layer):
        return (w_in_a, layer // 2) if layer % 2 == 0 else (w_in_b, layer // 2)

    xf = _perm(x.reshape(batch * seq, d))
    kv = _memkv(mem, w_mem_kv)
    w_in = w_in_a[0].astype(BF16)
    w_o = w_1 = w_2 = None
    for i in range(DEPTH):
        if i % 2 == 0:
            jobs = [(w_out, 0, 48), (w_ff1, 0, 64), (w_ff2, 0, 64)] if i == 0 else []
            qkv, casts = _proj(xf, w_in, a_cols, gelu=False, cast_jobs=jobs)
            if i == 0:
                w_o, w_1, w_2 = casts
            outs, lses = [], []
            for g, (window, dilation) in enumerate(A_PAIRS):
                o, lse = _attn_group(qkv, batch, seq, g, window, dilation)
                outs.append(o)
                lses.append(lse)
            xf = _mix_a(xf, outs, lses, seq, w_in, kv, w_o, ln1_g, ln1_b, i)
        else:
            z, _ = _proj(xf, w_in, b_cols, gelu=True)
            xf = _mix_b(xf, z, seq, ws_perm, bs_perm, vnorm_g, vnorm_b, i // 2, w_in, kv, w_o, ln1_g, ln1_b, i)
        if i + 1 < DEPTH:
            nxt_in, nxt_layer = w_in_of(i + 1)
            jobs = [(nxt_in, nxt_layer, 128), (w_out, i + 1, 96), (w_ff1, i + 1, 128), (w_ff2, i + 1, 128)]
            xf, (w_in, w_o, nxt_1, nxt_2) = _ffn(xf, w_1, w_2, ln2_g, ln2_b, i, cast_jobs=jobs)
            w_1, w_2 = nxt_1, nxt_2
        else:
            xf, _ = _ffn(xf, w_1, w_2, ln2_g, ln2_b, i)
    return _perm(xf).reshape(batch, seq, d)
```

```python
import functools

import numpy as np
import jax
import jax.numpy as jnp
from jax import lax
from jax.experimental import pallas as pl
from jax.experimental.pallas import tpu as pltpu

F32 = jnp.float32
BF16 = jnp.bfloat16

HEAD_DIM = 128
A_PAIRS = ((128, 1), (512, 4), (2048, 16))
A_GROUPS = len(A_PAIRS)
A_WINDOW = 128
B_CHUNK = 128
MEM_HEADS = 4
MEM_WIDTH = MEM_HEADS * HEAD_DIM
DEPTH = 4
ALPHA = (2.0 * DEPTH) ** 0.25
LN_EPS = 1e-5
SCALE = HEAD_DIM ** -0.5
SQRT_HALF = np.float32(np.sqrt(0.5))

LANES = 128
VMEM_LIMIT_BYTES = 56 * 1024 * 1024

RUN = 16
SUPER = RUN * RUN

PERM_ROWS = 4 * SUPER
PROJ_TM, PROJ_TN = 1024, 1024
ATTN_ROWS = 512
MIX_TM = 512
MIX_SUB = SUPER
FFN_TM, FFN_TF = 512, 1024

_NT = (((1,), (1,)), ((), ()))


def _params(*sem):
    return pltpu.CompilerParams(dimension_semantics=sem, vmem_limit_bytes=VMEM_LIMIT_BYTES)


def _cast_specs(jobs, step_of):
    in_specs, out_specs, out_shapes = [], [], []
    for src, layer, n_blocks in jobs:
        _, rows, cols = src.shape
        assert rows % n_blocks == 0 and (rows // n_blocks) % RUN == 0
        blk = rows // n_blocks
        in_specs.append(pl.BlockSpec(
            (1, blk, cols), lambda *ids, _l=layer, _n=n_blocks: (_l, jnp.minimum(step_of(*ids), _n - 1), 0)))
        out_specs.append(pl.BlockSpec(
            (blk, cols), lambda *ids, _n=n_blocks: (jnp.minimum(step_of(*ids), _n - 1), 0)))
        out_shapes.append(jax.ShapeDtypeStruct((rows, cols), BF16))
    return in_specs, out_specs, out_shapes


def _with_casts(body, n_in, n_out, n_cast):
    def kernel(*refs):
        ins, refs = refs[:n_in], refs[n_in:]
        cast_src, refs = refs[:n_cast], refs[n_cast:]
        outs, refs = refs[:n_out], refs[n_out:]
        cast_dst, scratch = refs[:n_cast], refs[n_cast:]

        def side_work():
            for src, dst in zip(cast_src, cast_dst):
                dst[...] = src[0].astype(BF16)

        body(*ins, *outs, *scratch, side_work=side_work)
    return kernel


def _layer_norm(x, g, b):
    mu = jnp.mean(x, axis=-1, keepdims=True)
    xc = x - mu
    var = jnp.mean(xc * xc, axis=-1, keepdims=True)
    return xc * lax.rsqrt(var + LN_EPS) * g + b


def _perm_kernel(x_ref, o_ref, slab_ref):
    n_slabs, rows, _ = slab_ref.shape
    for c in range(n_slabs):
        slab_ref[c] = x_ref[:, c * LANES:(c + 1) * LANES]

    def one_superblock(sb, carry):
        base = pl.multiple_of(sb * SUPER, SUPER)
        for r in range(RUN):
            for c in range(n_slabs):
                o_ref[pl.ds(base + r * RUN, RUN), c * LANES:(c + 1) * LANES] = (
                    slab_ref[c, pl.ds(base + r, RUN, stride=RUN), :])
        return carry

    lax.fori_loop(0, rows // SUPER, one_superblock, 0)


def _perm(x):
    t, d = x.shape
    return pl.pallas_call(
        _perm_kernel,
        grid=(t // PERM_ROWS,),
        in_specs=[pl.BlockSpec((PERM_ROWS, d), lambda i: (i, 0))],
        out_specs=pl.BlockSpec((PERM_ROWS, d), lambda i: (i, 0)),
        out_shape=jax.ShapeDtypeStruct((t, d), x.dtype),
        scratch_shapes=[pltpu.VMEM((d // LANES, PERM_ROWS, LANES), x.dtype)],
        compiler_params=_params("parallel"),
        name="perm",
    )(x)


def _memkv_kernel(mem_ref, w_ref, o_ref):
    o_ref[0, 0] = jnp.dot(mem_ref[0].astype(BF16), w_ref[0].astype(BF16),
                          preferred_element_type=F32).astype(BF16)


def _memkv(mem, w_mem_kv):
    nb, m, d = mem.shape
    depth, _, n = w_mem_kv.shape
    return pl.pallas_call(
        _memkv_kernel,
        grid=(depth, nb),
        in_specs=[pl.BlockSpec((1, m, d), lambda i, b: (b, 0, 0)),
                  pl.BlockSpec((1, d, n), lambda i, b: (i, 0, 0))],
        out_specs=pl.BlockSpec((1, 1, m, n), lambda i, b: (i, b, 0, 0)),
        out_shape=jax.ShapeDtypeStruct((depth, nb, m, n), BF16),
        compiler_params=_params("parallel", "arbitrary"),
        name="memkv",
    )(mem, w_mem_kv)


def _proj_kernel(x_ref, w_ref, o_ref, xb_ref, *, gelu, side_work):
    @pl.when(pl.program_id(1) == 0)
    def _():
        xb_ref[...] = x_ref[...].astype(BF16)

    side_work()
    acc = jnp.dot(xb_ref[...], w_ref[...], preferred_element_type=F32)
    if gelu:
        acc = 0.5 * acc * (1.0 + lax.erf(acc * SQRT_HALF))
    o_ref[...] = acc.astype(o_ref.dtype)


def _proj(x, w, n_out, gelu, cast_jobs=()):
    t, d = x.shape
    n_j = n_out // PROJ_TN
    c_in, c_out, c_shape = _cast_specs(cast_jobs, lambda i, j: i * n_j + j)
    out, *casts = pl.pallas_call(
        _with_casts(functools.partial(_proj_kernel, gelu=gelu), 2, 1, len(cast_jobs)),
        grid=(t // PROJ_TM, n_j),
        in_specs=[pl.BlockSpec((PROJ_TM, d), lambda i, j: (i, 0)),
                  pl.BlockSpec((d, PROJ_TN), lambda i, j: (0, j))] + c_in,
        out_specs=[pl.BlockSpec((PROJ_TM, PROJ_TN), lambda i, j: (i, j))] + c_out,
        out_shape=[jax.ShapeDtypeStruct((t, n_out), BF16)] + c_shape,
        scratch_shapes=[pltpu.VMEM((PROJ_TM, d), BF16)],
        compiler_params=_params("arbitrary", "arbitrary"),
        name="proj_gelu" if gelu else "proj",
    )(x, w, *[job[0] for job in cast_jobs])
    return out, casts


def _class_pos(i, dilation):
    per_super = SUPER // dilation
    n_a = RUN // dilation
    return (i & ~(per_super - 1)) + n_a * (i & (RUN - 1)) + ((i >> 4) & (n_a - 1))


def _attn_kernel(q_ref, kc_ref, kp_ref, vc_ref, vp_ref, o_ref, lse_ref,
                 qbuf, kbuf, vbuf, obuf, lbuf, bias_ref, *, rows, unit, tile, heads, dilation):
    n_keys = 2 * unit
    width = heads * HEAD_DIM
    chunk = pl.program_id(2)
    first_step = (pl.program_id(0) == 0) & (pl.program_id(1) == 0) & (chunk == 0)

    @pl.when(first_step)
    def _():
        neg_inf = jnp.float32(-jnp.inf)
        for qt in range(unit // tile):
            row = lax.broadcasted_iota(jnp.int32, (tile, n_keys), 0) + qt * tile
            col = lax.broadcasted_iota(jnp.int32, (tile, n_keys), 1)
            dist = unit + _class_pos(row, dilation) - (_class_pos(col & (unit - 1), dilation) + (col & unit))
            band = (dist >= 0) & (dist <= A_WINDOW)
            bias_ref[0, qt] = jnp.where(band & (col >= unit), 0.0, neg_inf)
            bias_ref[1, qt] = jnp.where(band, 0.0, neg_inf)

    qbuf[...] = q_ref[...].reshape(rows, width)
    kbuf[0:unit, :] = kp_ref[...].reshape(unit, width)
    kbuf[unit:, :] = kc_ref[...].reshape(rows, width)
    vbuf[0:unit, :] = vp_ref[...].reshape(unit, width)
    vbuf[unit:, :] = vc_ref[...].reshape(rows, width)

    has_prev = jnp.where(chunk > 0, 1, 0)
    lane = lax.broadcasted_iota(jnp.int32, (tile, LANES), 1)
    for u in range(rows // unit):
        k0 = u * unit
        for qt in range(unit // tile):
            bias = bias_ref[has_prev if u == 0 else 1, qt]
            r0 = k0 + qt * tile
            lse_tile = jnp.zeros((tile, LANES), F32)
            for h in range(heads):
                cs = slice(h * HEAD_DIM, (h + 1) * HEAD_DIM)
                s = lax.dot_general(qbuf[r0:r0 + tile, cs], kbuf[k0:k0 + n_keys, cs], _NT,
                                    preferred_element_type=F32) * SCALE + bias
                m = jnp.max(s, axis=-1, keepdims=True)
                p = jnp.exp(s - m)
                l = jnp.sum(p, axis=-1, keepdims=True)
                o = jnp.dot(p.astype(BF16), vbuf[k0:k0 + n_keys, cs], preferred_element_type=F32) / l
                obuf[r0:r0 + tile, cs] = o.astype(BF16)
                lse_tile = jnp.where(lane == h, m + jnp.log(l), lse_tile)
            lbuf[r0:r0 + tile, :] = lse_tile
    o_ref[...] = obuf[...].reshape(o_ref.shape)
    lse_ref[...] = lbuf[...].reshape(lse_ref.shape)


def _attn_group(qkv, batch, seq, group, window, dilation):
    t, cols = qkv.shape
    width = cols // (A_GROUPS * 3)
    heads = width // HEAD_DIM
    assert window // dilation == A_WINDOW and RUN % dilation == 0
    n_super = seq // SUPER
    n_a = RUN // dilation
    per_super = SUPER // dilation
    unit = max(A_WINDOW, per_super)
    rows = min(ATTN_ROWS, seq // dilation)
    assert rows % unit == 0 and unit % per_super == 0
    tile = unit

    def view(a, last):
        return a.reshape(batch, n_super, n_a, dilation, RUN, last)

    def cur(last, col_block):
        return pl.BlockSpec((None, rows // per_super, n_a, None, RUN, last),
                            lambda b, r, c: (b, c, 0, r, 0, col_block))

    def prev(col_block):
        return pl.BlockSpec((None, unit // per_super, n_a, None, RUN, width),
                            lambda b, r, c: (b, jnp.maximum(c * (rows // unit) - 1, 0), 0, r, 0, col_block))

    qkv_v = view(qkv, cols)
    q_col, k_col, v_col = (group * 3 + i for i in range(3))
    o, lse = pl.pallas_call(
        functools.partial(_attn_kernel, rows=rows, unit=unit, tile=tile, heads=heads, dilation=dilation),
        grid=(batch, dilation, (seq // dilation) // rows),
        in_specs=[cur(width, q_col), cur(width, k_col), prev(k_col), cur(width, v_col), prev(v_col)],
        out_specs=[cur(width, 0), cur(LANES, 0)],
        out_shape=[jax.ShapeDtypeStruct((batch, n_super, n_a, dilation, RUN, width), BF16),
                   jax.ShapeDtypeStruct((batch, n_super, n_a, dilation, RUN, LANES), F32)],
        scratch_shapes=[pltpu.VMEM((rows, width), BF16),
                        pltpu.VMEM((rows + unit, width), BF16),
                        pltpu.VMEM((rows + unit, width), BF16),
                        pltpu.VMEM((rows, width), BF16),
                        pltpu.VMEM((rows, LANES), F32),
                        pltpu.VMEM((2, unit // tile, tile, 2 * unit), F32)],
        compiler_params=_params("arbitrary", "arbitrary", "arbitrary"),
        name=f"attn_d{dilation}",
    )(qkv_v, qkv_v, qkv_v, qkv_v, qkv_v)
    return o.reshape(t, width), lse.reshape(t, LANES)


def _mem_attn_outproj_ln(rs, x_ref, wmq_ref, kv_ref, wout_ref, g_ref, b_ref, out_ref, outb_ref, cat_ref,
                         mix_width):
    xf = x_ref[rs, :]
    mq = jnp.dot(xf.astype(BF16), wmq_ref[...], preferred_element_type=F32).astype(BF16)
    for h in range(MEM_HEADS):
        c0 = h * HEAD_DIM
        k = kv_ref[0, 0, :, c0:c0 + HEAD_DIM]
        v = kv_ref[0, 0, :, MEM_WIDTH + c0:MEM_WIDTH + c0 + HEAD_DIM]
        s = lax.dot_general(mq[:, c0:c0 + HEAD_DIM], k, _NT, preferred_element_type=F32) * SCALE
        p = jnp.exp(s - jnp.max(s, axis=-1, keepdims=True))
        l = jnp.sum(p, axis=-1, keepdims=True)
        o = jnp.dot(p.astype(BF16), v, preferred_element_type=F32) / l
        cat_ref[rs, mix_width + c0:mix_width + c0 + HEAD_DIM] = o.astype(BF16)
    y = jnp.dot(cat_ref[rs, :], wout_ref[...], preferred_element_type=F32)
    x1 = _layer_norm(ALPHA * xf + y, g_ref[0], b_ref[0])
    out_ref[rs, :] = x1
    outb_ref[rs, :] = x1.astype(BF16)


def _row_blocks(tm):
    return [slice(r, r + MIX_SUB) for r in range(0, tm, MIX_SUB)]


def _mix_a_kernel(o0_ref, o1_ref, o2_ref, l0_ref, l1_ref, l2_ref, x_ref, wmq_ref, kv_ref, wout_ref,
                  g_ref, b_ref, out_ref, outb_ref, cat_ref, *, heads):
    for rs in _row_blocks(x_ref.shape[0]):
        l0, l1, l2 = l0_ref[rs, :], l1_ref[rs, :], l2_ref[rs, :]
        mx = jnp.maximum(jnp.maximum(l0, l1), l2)
        e0, e1, e2 = jnp.exp(l0 - mx), jnp.exp(l1 - mx), jnp.exp(l2 - mx)
        den = e0 + e1 + e2
        w0, w1, w2 = e0 / den, e1 / den, e2 / den
        for h in range(heads):
            sl = slice(h * HEAD_DIM, (h + 1) * HEAD_DIM)
            mix = (w0[:, h:h + 1] * o0_ref[rs, sl].astype(F32)
                   + w1[:, h:h + 1] * o1_ref[rs, sl].astype(F32)
                   + w2[:, h:h + 1] * o2_ref[rs, sl].astype(F32))
            cat_ref[rs, sl] = mix.astype(BF16)
        _mem_attn_outproj_ln(rs, x_ref, wmq_ref, kv_ref, wout_ref, g_ref, b_ref, out_ref, outb_ref, cat_ref,
                             heads * HEAD_DIM)


def _mix_b_kernel(u_ref, v_ref, ws_ref, bs_ref, vg_ref, vb_ref, x_ref, wmq_ref, kv_ref, wout_ref,
                  g_ref, b_ref, out_ref, outb_ref, cat_ref, vn_ref, wsp_ref, *, groups):
    tm = u_ref.shape[0]

    @pl.when(pl.program_id(0) == 0)
    def _():
        row = lax.broadcasted_iota(jnp.int32, (SUPER, SUPER), 0)
        col = lax.broadcasted_iota(jnp.int32, (SUPER, SUPER), 1)
        nat_row = RUN * (row & (RUN - 1)) + (row >> 4)
        nat_col = RUN * (col & (RUN - 1)) + (col >> 4)
        same_chunk = (nat_row // B_CHUNK) == (nat_col // B_CHUNK)
        causal = (same_chunk & (nat_col <= nat_row)).astype(F32)
        i = lax.broadcasted_iota(jnp.int32, (SUPER, B_CHUNK), 0)
        p = lax.broadcasted_iota(jnp.int32, (SUPER, B_CHUNK), 1)
        pos = (RUN * (i & (RUN - 1)) + (i >> 4)) % B_CHUNK
        pick = (p == pos).astype(BF16)
        for g in range(groups):
            rows = jnp.dot(pick, ws_ref[0, g].astype(BF16), preferred_element_type=F32).astype(BF16)
            full = lax.dot_general(rows, pick, _NT, preferred_element_type=F32)
            wsp_ref[g] = (full * causal).astype(BF16)

    for rs in _row_blocks(tm):
        vn_ref[rs, :] = _layer_norm(v_ref[rs, :].astype(F32), vg_ref[0], vb_ref[0]).astype(BF16)
        for g in range(groups):
            sl = slice(g * HEAD_DIM, (g + 1) * HEAD_DIM)
            sg = jnp.dot(wsp_ref[g], vn_ref[rs, sl], preferred_element_type=F32) + bs_ref[0, :, g:g + 1]
            cat_ref[rs, sl] = (u_ref[rs, sl].astype(F32) * sg).astype(BF16)
        _mem_attn_outproj_ln(rs, x_ref, wmq_ref, kv_ref, wout_ref, g_ref, b_ref, out_ref, outb_ref, cat_ref,
                             groups * HEAD_DIM)


def _mix_tail_in_specs(d, seq, w_in_shape, layer, kv_shape, wout_shape):
    tiles_per_batch = seq // MIX_TM
    _, _, m, kvw = kv_shape
    k_out, _ = wout_shape
    mq_block = (w_in_shape[1] - MEM_WIDTH) // MEM_WIDTH
    return [
        pl.BlockSpec((MIX_TM, d), lambda i: (i, 0)),
        pl.BlockSpec((d, MEM_WIDTH), lambda i: (0, mq_block)),
        pl.BlockSpec((1, 1, m, kvw), lambda i: (layer, i // tiles_per_batch, 0, 0)),
        pl.BlockSpec((k_out, d), lambda i: (0, 0)),
        pl.BlockSpec((1, 1, d), lambda i: (layer, 0, 0)),
        pl.BlockSpec((1, 1, d), lambda i: (layer, 0, 0)),
    ]


def _mix_a(x, outs, lses, seq, w_in, kv, w_out, ln_g, ln_b, layer):
    t, d = x.shape
    width = outs[0].shape[1]
    tail = _mix_tail_in_specs(d, seq, w_in.shape, layer, kv.shape, w_out.shape)
    return pl.pallas_call(
        functools.partial(_mix_a_kernel, heads=width // HEAD_DIM),
        grid=(t // MIX_TM,),
        in_specs=[pl.BlockSpec((MIX_TM, width), lambda i: (i, 0))] * 3
                 + [pl.BlockSpec((MIX_TM, LANES), lambda i: (i, 0))] * 3 + tail,
        out_specs=[pl.BlockSpec((MIX_TM, d), lambda i: (i, 0))] * 2,
        out_shape=[jax.ShapeDtypeStruct((t, d), F32), jax.ShapeDtypeStruct((t, d), BF16)],
        scratch_shapes=[pltpu.VMEM((MIX_TM, w_out.shape[0]), BF16)],
        compiler_params=_params("parallel"),
        name="mix_a",
    )(*outs, *lses, x, w_in, kv, w_out, ln_g, ln_b)


def _mix_b(x, z, seq, w_s, bs_perm, vn_g, vn_b, layer_in, w_in, kv, w_out, ln_g, ln_b, layer):
    t, d = x.shape
    width = z.shape[1] // 2
    groups = width // HEAD_DIM
    tail = _mix_tail_in_specs(d, seq, w_in.shape, layer, kv.shape, w_out.shape)
    return pl.pallas_call(
        functools.partial(_mix_b_kernel, groups=groups),
        grid=(t // MIX_TM,),
        in_specs=[pl.BlockSpec((MIX_TM, width), lambda i: (i, 0)),
                  pl.BlockSpec((MIX_TM, width), lambda i: (i, 1)),
                  pl.BlockSpec((1, groups, B_CHUNK, B_CHUNK), lambda i: (layer_in, 0, 0, 0)),
                  pl.BlockSpec((1, SUPER, groups), lambda i: (layer_in, 0, 0)),
                  pl.BlockSpec((1, 1, width), lambda i: (layer_in, 0, 0)),
                  pl.BlockSpec((1, 1, width), lambda i: (layer_in, 0, 0))] + tail,
        out_specs=[pl.BlockSpec((MIX_TM, d), lambda i: (i, 0))] * 2,
        out_shape=[jax.ShapeDtypeStruct((t, d), F32), jax.ShapeDtypeStruct((t, d), BF16)],
        scratch_shapes=[pltpu.VMEM((MIX_TM, w_out.shape[0]), BF16),
                        pltpu.VMEM((MIX_TM, width), BF16),
                        pltpu.VMEM((groups, SUPER, SUPER), BF16)],
        compiler_params=_params("arbitrary"),
        name="mix_b",
    )(z, z, w_s, bs_perm, vn_g, vn_b, x, w_in, kv, w_out, ln_g, ln_b)


def _ffn_kernel(x_ref, xb_ref, w1_ref, w2_ref, g_ref, b_ref, o_ref, *, side_work):
    j = pl.program_id(1)

    @pl.when(j == 0)
    def _():
        o_ref[...] = jnp.zeros_like(o_ref)

    side_work()
    h = jnp.maximum(jnp.dot(xb_ref[...], w1_ref[...], preferred_element_type=F32), 0.0)
    o_ref[...] += jnp.dot((h * h).astype(BF16), w2_ref[...], preferred_element_type=F32)

    @pl.when(j == pl.num_programs(1) - 1)
    def _():
        o_ref[...] = _layer_norm(ALPHA * x_ref[...] + o_ref[...], g_ref[0], b_ref[0])


def _ffn(x, xb, w1, w2, ln_g, ln_b, layer, cast_jobs=()):
    t, d = x.shape
    d_ff = w1.shape[1]
    n_j = d_ff // FFN_TF
    c_in, c_out, c_shape = _cast_specs(cast_jobs, lambda i, j: i * n_j + j)
    out, *casts = pl.pallas_call(
        _with_casts(_ffn_kernel, 6, 1, len(cast_jobs)),
        grid=(t // FFN_TM, n_j),
        in_specs=[pl.BlockSpec((FFN_TM, d), lambda i, j: (i, 0)),
                  pl.BlockSpec((FFN_TM, d), lambda i, j: (i, 0)),
                  pl.BlockSpec((d, FFN_TF), lambda i, j: (0, j)),
                  pl.BlockSpec((FFN_TF, d), lambda i, j: (j, 0)),
                  pl.BlockSpec((1, 1, d), lambda i, j: (layer, 0, 0)),
                  pl.BlockSpec((1, 1, d), lambda i, j: (layer, 0, 0))] + c_in,
        out_specs=[pl.BlockSpec((FFN_TM, d), lambda i, j: (i, 0))] + c_out,
        out_shape=[jax.ShapeDtypeStruct((t, d), F32)] + c_shape,
        compiler_params=_params("arbitrary", "arbitrary"),
        name="ffn",
    )(x, xb, w1, w2, ln_g, ln_b, *[job[0] for job in cast_jobs])
    return out, casts


def _gmlp_bias_in_working_order(b_s):
    n_layers, groups, chunk = b_s.shape
    half = chunk // RUN
    reps = SUPER // chunk
    b = b_s.reshape(n_layers, groups, half, RUN).transpose(0, 3, 2, 1)
    b = jnp.broadcast_to(b[:, :, None], (n_layers, RUN, reps, half, groups))
    return b.reshape(n_layers, SUPER, groups)


def kernel(x, mem, w_in_a, w_in_b, w_s, b_s, vnorm_g, vnorm_b, w_mem_kv, w_out, ln1_g, ln1_b, w_ff1, w_ff2,
           ln2_g, ln2_b):
    batch, seq, d = x.shape
    row3 = lambda a: a.reshape(a.shape[0], 1, a.shape[1])
    ln1_g, ln1_b, ln2_g, ln2_b = row3(ln1_g), row3(ln1_b), row3(ln2_g), row3(ln2_b)
    vnorm_g, vnorm_b = row3(vnorm_g), row3(vnorm_b)
    bs_perm = _gmlp_bias_in_working_order(b_s)
    a_cols = w_in_a.shape[2] - MEM_WIDTH
    b_cols = w_in_b.shape[2] - MEM_WIDTH

    def w_in_of(layer):
        return (w_in_a, layer // 2) if layer % 2 == 0 else (w_in_b, layer // 2)

    xf = _perm(x.reshape(batch * seq, d))
    kv = _memkv(mem, w_mem_kv)
    w_in = w_in_a[0].astype(BF16)
    w_o = w_1 = w_2 = None
    for i in range(DEPTH):
        if i % 2 == 0:
            jobs = [(w_out, 0, 48), (w_ff1, 0, 64), (w_ff2, 0, 64)] if i == 0 else []
            qkv, casts = _proj(xf, w_in, a_cols, gelu=False, cast_jobs=jobs)
            if i == 0:
                w_o, w_1, w_2 = casts
            outs, lses = [], []
            for g, (window, dilation) in enumerate(A_PAIRS):
                o, lse = _attn_group(qkv, batch, seq, g, window, dilation)
                outs.append(o)
                lses.append(lse)
            xf, xb = _mix_a(xf, outs, lses, seq, w_in, kv, w_o, ln1_g, ln1_b, i)
        else:
            z, _ = _proj(xf, w_in, b_cols, gelu=True)
            xf, xb = _mix_b(xf, z, seq, w_s, bs_perm, vnorm_g, vnorm_b, i // 2, w_in, kv, w_o, ln1_g, ln1_b, i)
        if i + 1 < DEPTH:
            nxt_in, nxt_layer = w_in_of(i + 1)
            jobs = [(nxt_in, nxt_layer, 128), (w_out, i + 1, 96), (w_ff1, i + 1, 128), (w_ff2, i + 1, 128)]
            xf, (w_in, w_o, w_1, w_2) = _ffn(xf, xb, w_1, w_2, ln2_g, ln2_b, i, cast_jobs=jobs)
        else:
            xf, _ = _ffn(xf, xb, w_1, w_2, ln2_g, ln2_b, i)
    return _perm(xf).reshape(batch, seq, d)
```

```python
import functools

import numpy as np
import jax
import jax.numpy as jnp
from jax import lax
from jax.experimental import pallas as pl
from jax.experimental.pallas import tpu as pltpu

F32 = jnp.float32
BF16 = jnp.bfloat16

HEAD_DIM = 128
A_PAIRS = ((128, 1), (512, 4), (2048, 16))
A_GROUPS = len(A_PAIRS)
A_WINDOW = 128
B_CHUNK = 128
MEM_HEADS = 4
MEM_WIDTH = MEM_HEADS * HEAD_DIM
DEPTH = 4
ALPHA = (2.0 * DEPTH) ** 0.25
LN_EPS = 1e-5
SCALE = HEAD_DIM ** -0.5
SQRT_HALF = np.float32(np.sqrt(0.5))

LANES = 128
VMEM_LIMIT_BYTES = 56 * 1024 * 1024

RUN = 16
SUPER = RUN * RUN

PERM_ROWS = 4 * SUPER
PROJ_TM, PROJ_TN = 1024, 1024
ATTN_ROWS = 512
MIX_TM = 512
MIX_SUB = MIX_TM
FFN_TM, FFN_TF = 1024, 512

_NT = (((1,), (1,)), ((), ()))


def _params(*sem):
    return pltpu.CompilerParams(dimension_semantics=sem, vmem_limit_bytes=VMEM_LIMIT_BYTES)


def _cast_specs(jobs, step_of):
    in_specs, out_specs, out_shapes = [], [], []
    for src, layer, n_blocks in jobs:
        _, rows, cols = src.shape
        assert rows % n_blocks == 0 and (rows // n_blocks) % RUN == 0
        blk = rows // n_blocks
        in_specs.append(pl.BlockSpec(
            (1, blk, cols), lambda *ids, _l=layer, _n=n_blocks: (_l, jnp.minimum(step_of(*ids), _n - 1), 0)))
        out_specs.append(pl.BlockSpec(
            (blk, cols), lambda *ids, _n=n_blocks: (jnp.minimum(step_of(*ids), _n - 1), 0)))
        out_shapes.append(jax.ShapeDtypeStruct((rows, cols), BF16))
    return in_specs, out_specs, out_shapes


def _with_casts(body, n_in, n_out, n_cast):
    def kernel(*refs):
        ins, refs = refs[:n_in], refs[n_in:]
        cast_src, refs = refs[:n_cast], refs[n_cast:]
        outs, refs = refs[:n_out], refs[n_out:]
        cast_dst, scratch = refs[:n_cast], refs[n_cast:]

        def side_work():
            for src, dst in zip(cast_src, cast_dst):
                dst[...] = src[0].astype(BF16)

        body(*ins, *outs, *scratch, side_work=side_work)
    return kernel


def _layer_norm(x, g, b):
    mu = jnp.mean(x, axis=-1, keepdims=True)
    xc = x - mu
    var = jnp.mean(xc * xc, axis=-1, keepdims=True)
    return xc * lax.rsqrt(var + LN_EPS) * g + b


def _perm_kernel(x_ref, o_ref, slab_ref):
    n_slabs, rows, _ = slab_ref.shape
    for c in range(n_slabs):
        slab_ref[c] = x_ref[:, c * LANES:(c + 1) * LANES]

    def one_superblock(sb, carry):
        base = pl.multiple_of(sb * SUPER, SUPER)
        for r in range(RUN):
            for c in range(n_slabs):
                o_ref[pl.ds(base + r * RUN, RUN), c * LANES:(c + 1) * LANES] = (
                    slab_ref[c, pl.ds(base + r, RUN, stride=RUN), :])
        return carry

    lax.fori_loop(0, rows // SUPER, one_superblock, 0)


def _perm(x):
    t, d = x.shape
    return pl.pallas_call(
        _perm_kernel,
        grid=(t // PERM_ROWS,),
        in_specs=[pl.BlockSpec((PERM_ROWS, d), lambda i: (i, 0))],
        out_specs=pl.BlockSpec((PERM_ROWS, d), lambda i: (i, 0)),
        out_shape=jax.ShapeDtypeStruct((t, d), x.dtype),
        scratch_shapes=[pltpu.VMEM((d // LANES, PERM_ROWS, LANES), x.dtype)],
        compiler_params=_params("parallel"),
        name="perm",
    )(x)


def _memkv_kernel(mem_ref, w_ref, o_ref):
    o_ref[0, 0] = jnp.dot(mem_ref[0].astype(BF16), w_ref[0].astype(BF16),
                          preferred_element_type=F32).astype(BF16)


def _memkv(mem, w_mem_kv):
    nb, m, d = mem.shape
    depth, _, n = w_mem_kv.shape
    return pl.pallas_call(
        _memkv_kernel,
        grid=(depth, nb),
        in_specs=[pl.BlockSpec((1, m, d), lambda i, b: (b, 0, 0)),
                  pl.BlockSpec((1, d, n), lambda i, b: (i, 0, 0))],
        out_specs=pl.BlockSpec((1, 1, m, n), lambda i, b: (i, b, 0, 0)),
        out_shape=jax.ShapeDtypeStruct((depth, nb, m, n), BF16),
        compiler_params=_params("parallel", "arbitrary"),
        name="memkv",
    )(mem, w_mem_kv)


def _proj_kernel(x_ref, w_ref, o_ref, xb_ref, *, gelu, side_work):
    @pl.when(pl.program_id(1) == 0)
    def _():
        xb_ref[...] = x_ref[...].astype(BF16)

    side_work()
    acc = jnp.dot(xb_ref[...], w_ref[...], preferred_element_type=F32)
    if gelu:
        acc = 0.5 * acc * (1.0 + lax.erf(acc * SQRT_HALF))
    o_ref[...] = acc.astype(o_ref.dtype)


def _proj(x, w, n_out, gelu, cast_jobs=()):
    t, d = x.shape
    n_j = n_out // PROJ_TN
    c_in, c_out, c_shape = _cast_specs(cast_jobs, lambda i, j: i * n_j + j)
    out, *casts = pl.pallas_call(
        _with_casts(functools.partial(_proj_kernel, gelu=gelu), 2, 1, len(cast_jobs)),
        grid=(t // PROJ_TM, n_j),
        in_specs=[pl.BlockSpec((PROJ_TM, d), lambda i, j: (i, 0)),
                  pl.BlockSpec((d, PROJ_TN), lambda i, j: (0, j))] + c_in,
        out_specs=[pl.BlockSpec((PROJ_TM, PROJ_TN), lambda i, j: (i, j))] + c_out,
        out_shape=[jax.ShapeDtypeStruct((t, n_out), BF16)] + c_shape,
        scratch_shapes=[pltpu.VMEM((PROJ_TM, d), BF16)],
        compiler_params=_params("arbitrary", "arbitrary"),
        name="proj_gelu" if gelu else "proj",
    )(x, w, *[job[0] for job in cast_jobs])
    return out, casts


def _class_pos(i, dilation):
    per_super = SUPER // dilation
    n_a = RUN // dilation
    return (i & ~(per_super - 1)) + n_a * (i & (RUN - 1)) + ((i >> 4) & (n_a - 1))


def _attn_kernel(q_ref, kc_ref, kp_ref, vc_ref, vp_ref, o_ref, lse_ref,
                 qbuf, kbuf, vbuf, obuf, lbuf, bias_ref, *, rows, unit, tile, heads, dilation):
    n_keys = 2 * unit
    width = heads * HEAD_DIM
    chunk = pl.program_id(2)
    first_step = (pl.program_id(0) == 0) & (pl.program_id(1) == 0) & (chunk == 0)

    @pl.when(first_step)
    def _():
        neg_inf = jnp.float32(-jnp.inf)
        for qt in range(unit // tile):
            row = lax.broadcasted_iota(jnp.int32, (tile, n_keys), 0) + qt * tile
            col = lax.broadcasted_iota(jnp.int32, (tile, n_keys), 1)
            dist = unit + _class_pos(row, dilation) - (_class_pos(col & (unit - 1), dilation) + (col & unit))
            band = (dist >= 0) & (dist <= A_WINDOW)
            bias_ref[0, qt] = jnp.where(band & (col >= unit), 0.0, neg_inf)
            bias_ref[1, qt] = jnp.where(band, 0.0, neg_inf)

    qbuf[...] = q_ref[...].reshape(rows, width)
    kbuf[0:unit, :] = kp_ref[...].reshape(unit, width)
    kbuf[unit:, :] = kc_ref[...].reshape(rows, width)
    vbuf[0:unit, :] = vp_ref[...].reshape(unit, width)
    vbuf[unit:, :] = vc_ref[...].reshape(rows, width)

    has_prev = jnp.where(chunk > 0, 1, 0)
    lane = lax.broadcasted_iota(jnp.int32, (tile, LANES), 1)
    for u in range(rows // unit):
        k0 = u * unit
        for qt in range(unit // tile):
            bias = bias_ref[has_prev if u == 0 else 1, qt]
            r0 = k0 + qt * tile
            lse_tile = jnp.zeros((tile, LANES), F32)
            for h in range(heads):
                cs = slice(h * HEAD_DIM, (h + 1) * HEAD_DIM)
                s = lax.dot_general(qbuf[r0:r0 + tile, cs], kbuf[k0:k0 + n_keys, cs], _NT,
                                    preferred_element_type=F32) * SCALE + bias
                m = jnp.max(s, axis=-1, keepdims=True)
                p = jnp.exp(s - m)
                l = jnp.sum(p, axis=-1, keepdims=True)
                o = jnp.dot(p.astype(BF16), vbuf[k0:k0 + n_keys, cs], preferred_element_type=F32) / l
                obuf[r0:r0 + tile, cs] = o.astype(BF16)
                lse_tile = jnp.where(lane == h, m + jnp.log(l), lse_tile)
            lbuf[r0:r0 + tile, :] = lse_tile
    o_ref[...] = obuf[...].reshape(o_ref.shape)
    lse_ref[...] = lbuf[...].reshape(lse_ref.shape)


def _attn_group(qkv, batch, seq, group, window, dilation):
    t, cols = qkv.shape
    width = cols // (A_GROUPS * 3)
    heads = width // HEAD_DIM
    assert window // dilation == A_WINDOW and RUN % dilation == 0
    n_super = seq // SUPER
    n_a = RUN // dilation
    per_super = SUPER // dilation
    unit = max(A_WINDOW, per_super)
    rows = min(ATTN_ROWS, seq // dilation)
    assert rows % unit == 0 and unit % per_super == 0
    tile = unit

    def view(a, last):
        return a.reshape(batch, n_super, n_a, dilation, RUN, last)

    def cur(last, col_block):
        return pl.BlockSpec((None, rows // per_super, n_a, None, RUN, last),
                            lambda b, r, c: (b, c, 0, r, 0, col_block))

    def prev(col_block):
        return pl.BlockSpec((None, unit // per_super, n_a, None, RUN, width),
                            lambda b, r, c: (b, jnp.maximum(c * (rows // unit) - 1, 0), 0, r, 0, col_block))

    qkv_v = view(qkv, cols)
    q_col, k_col, v_col = (group * 3 + i for i in range(3))
    o, lse = pl.pallas_call(
        functools.partial(_attn_kernel, rows=rows, unit=unit, tile=tile, heads=heads, dilation=dilation),
        grid=(batch, dilation, (seq // dilation) // rows),
        in_specs=[cur(width, q_col), cur(width, k_col), prev(k_col), cur(width, v_col), prev(v_col)],
        out_specs=[cur(width, 0), cur(LANES, 0)],
        out_shape=[jax.ShapeDtypeStruct((batch, n_super, n_a, dilation, RUN, width), BF16),
                   jax.ShapeDtypeStruct((batch, n_super, n_a, dilation, RUN, LANES), F32)],
        scratch_shapes=[pltpu.VMEM((rows, width), BF16),
                        pltpu.VMEM((rows + unit, width), BF16),
                        pltpu.VMEM((rows + unit, width), BF16),
                        pltpu.VMEM((rows, width), BF16),
                        pltpu.VMEM((rows, LANES), F32),
                        pltpu.VMEM((2, unit // tile, tile, 2 * unit), F32)],
        compiler_params=_params("arbitrary", "arbitrary", "arbitrary"),
        name=f"attn_d{dilation}",
    )(qkv_v, qkv_v, qkv_v, qkv_v, qkv_v)
    return o.reshape(t, width), lse.reshape(t, LANES)


def _mem_attn_outproj_ln(rs, x_ref, wmq_ref, kv_ref, wout_ref, g_ref, b_ref, out_ref, outb_ref, cat_ref,
                         mix_width):
    xf = x_ref[rs, :]
    mq = jnp.dot(xf.astype(BF16), wmq_ref[...], preferred_element_type=F32).astype(BF16)
    for h in range(MEM_HEADS):
        c0 = h * HEAD_DIM
        k = kv_ref[0, 0, :, c0:c0 + HEAD_DIM]
        v = kv_ref[0, 0, :, MEM_WIDTH + c0:MEM_WIDTH + c0 + HEAD_DIM]
        s = lax.dot_general(mq[:, c0:c0 + HEAD_DIM], k, _NT, preferred_element_type=F32) * SCALE
        p = jnp.exp(s - jnp.max(s, axis=-1, keepdims=True))
        l = jnp.sum(p, axis=-1, keepdims=True)
        o = jnp.dot(p.astype(BF16), v, preferred_element_type=F32) / l
        cat_ref[rs, mix_width + c0:mix_width + c0 + HEAD_DIM] = o.astype(BF16)
    y = jnp.dot(cat_ref[rs, :], wout_ref[...], preferred_element_type=F32)
    x1 = _layer_norm(ALPHA * xf + y, g_ref[0], b_ref[0])
    out_ref[rs, :] = x1
    outb_ref[rs, :] = x1.astype(BF16)


def _row_blocks(tm):
    return [slice(r, r + MIX_SUB) for r in range(0, tm, MIX_SUB)]


def _mix_a_kernel(o0_ref, o1_ref, o2_ref, l0_ref, l1_ref, l2_ref, x_ref, wmq_ref, kv_ref, wout_ref,
                  g_ref, b_ref, out_ref, outb_ref, cat_ref, *, heads):
    for rs in _row_blocks(x_ref.shape[0]):
        l0, l1, l2 = l0_ref[rs, :], l1_ref[rs, :], l2_ref[rs, :]
        mx = jnp.maximum(jnp.maximum(l0, l1), l2)
        e0, e1, e2 = jnp.exp(l0 - mx), jnp.exp(l1 - mx), jnp.exp(l2 - mx)
        den = e0 + e1 + e2
        w0, w1, w2 = e0 / den, e1 / den, e2 / den
        for h in range(heads):
            sl = slice(h * HEAD_DIM, (h + 1) * HEAD_DIM)
            mix = (w0[:, h:h + 1] * o0_ref[rs, sl].astype(F32)
                   + w1[:, h:h + 1] * o1_ref[rs, sl].astype(F32)
                   + w2[:, h:h + 1] * o2_ref[rs, sl].astype(F32))
            cat_ref[rs, sl] = mix.astype(BF16)
        _mem_attn_outproj_ln(rs, x_ref, wmq_ref, kv_ref, wout_ref, g_ref, b_ref, out_ref, outb_ref, cat_ref,
                             heads * HEAD_DIM)


def _mix_b_kernel(u_ref, v_ref, ws_ref, bs_ref, vg_ref, vb_ref, x_ref, wmq_ref, kv_ref, wout_ref,
                  g_ref, b_ref, out_ref, outb_ref, cat_ref, vn_ref, wsp_ref, *, groups):
    tm = u_ref.shape[0]

    @pl.when(pl.program_id(0) == 0)
    def _():
        row = lax.broadcasted_iota(jnp.int32, (SUPER, SUPER), 0)
        col = lax.broadcasted_iota(jnp.int32, (SUPER, SUPER), 1)
        nat_row = RUN * (row & (RUN - 1)) + (row >> 4)
        nat_col = RUN * (col & (RUN - 1)) + (col >> 4)
        same_chunk = (nat_row // B_CHUNK) == (nat_col // B_CHUNK)
        causal = (same_chunk & (nat_col <= nat_row)).astype(F32)
        i = lax.broadcasted_iota(jnp.int32, (SUPER, B_CHUNK), 0)
        p = lax.broadcasted_iota(jnp.int32, (SUPER, B_CHUNK), 1)
        pos = (RUN * (i & (RUN - 1)) + (i >> 4)) % B_CHUNK
        pick = (p == pos).astype(BF16)
        for g in range(groups):
            rows = jnp.dot(pick, ws_ref[0, g].astype(BF16), preferred_element_type=F32).astype(BF16)
            full = lax.dot_general(rows, pick, _NT, preferred_element_type=F32)
            wsp_ref[g] = (full * causal).astype(BF16)

    for rs in _row_blocks(tm):
        vn_ref[rs, :] = _layer_norm(v_ref[rs, :].astype(F32), vg_ref[0], vb_ref[0]).astype(BF16)
        for g in range(groups):
            sl = slice(g * HEAD_DIM, (g + 1) * HEAD_DIM)
            for s0 in range(rs.start, rs.stop, SUPER):
                sb = slice(s0, s0 + SUPER)
                sg = jnp.dot(wsp_ref[g], vn_ref[sb, sl], preferred_element_type=F32) + bs_ref[0, :, g:g + 1]
                cat_ref[sb, sl] = (u_ref[sb, sl].astype(F32) * sg).astype(BF16)
        _mem_attn_outproj_ln(rs, x_ref, wmq_ref, kv_ref, wout_ref, g_ref, b_ref, out_ref, outb_ref, cat_ref,
                             groups * HEAD_DIM)


def _mix_tail_in_specs(d, seq, w_in_shape, layer, kv_shape, wout_shape):
    tiles_per_batch = seq // MIX_TM
    _, _, m, kvw = kv_shape
    k_out, _ = wout_shape
    mq_block = (w_in_shape[1] - MEM_WIDTH) // MEM_WIDTH
    return [
        pl.BlockSpec((MIX_TM, d), lambda i: (i, 0)),
        pl.BlockSpec((d, MEM_WIDTH), lambda i: (0, mq_block)),
        pl.BlockSpec((1, 1, m, kvw), lambda i: (layer, i // tiles_per_batch, 0, 0)),
        pl.BlockSpec((k_out, d), lambda i: (0, 0)),
        pl.BlockSpec((1, 1, d), lambda i: (layer, 0, 0)),
        pl.BlockSpec((1, 1, d), lambda i: (layer, 0, 0)),
    ]


def _mix_a(x, outs, lses, seq, w_in, kv, w_out, ln_g, ln_b, layer):
    t, d = x.shape
    width = outs[0].shape[1]
    tail = _mix_tail_in_specs(d, seq, w_in.shape, layer, kv.shape, w_out.shape)
    return pl.pallas_call(
        functools.partial(_mix_a_kernel, heads=width // HEAD_DIM),
        grid=(t // MIX_TM,),
        in_specs=[pl.BlockSpec((MIX_TM, width), lambda i: (i, 0))] * 3
                 + [pl.BlockSpec((MIX_TM, LANES), lambda i: (i, 0))] * 3 + tail,
        out_specs=[pl.BlockSpec((MIX_TM, d), lambda i: (i, 0))] * 2,
        out_shape=[jax.ShapeDtypeStruct((t, d), F32), jax.ShapeDtypeStruct((t, d), BF16)],
        scratch_shapes=[pltpu.VMEM((MIX_TM, w_out.shape[0]), BF16)],
        compiler_params=_params("parallel"),
        name="mix_a",
    )(*outs, *lses, x, w_in, kv, w_out, ln_g, ln_b)


def _mix_b(x, z, seq, w_s, bs_perm, vn_g, vn_b, layer_in, w_in, kv, w_out, ln_g, ln_b, layer):
    t, d = x.shape
    width = z.shape[1] // 2
    groups = width // HEAD_DIM
    tail = _mix_tail_in_specs(d, seq, w_in.shape, layer, kv.shape, w_out.shape)
    return pl.pallas_call(
        functools.partial(_mix_b_kernel, groups=groups),
        grid=(t // MIX_TM,),
        in_specs=[pl.BlockSpec((MIX_TM, width), lambda i: (i, 0)),
                  pl.BlockSpec((MIX_TM, width), lambda i: (i, 1)),
                  pl.BlockSpec((1, groups, B_CHUNK, B_CHUNK), lambda i: (layer_in, 0, 0, 0)),
                  pl.BlockSpec((1, SUPER, groups), lambda i: (layer_in, 0, 0)),
                  pl.BlockSpec((1, 1, width), lambda i: (layer_in, 0, 0)),
                  pl.BlockSpec((1, 1, width), lambda i: (layer_in, 0, 0))] + tail,
        out_specs=[pl.BlockSpec((MIX_TM, d), lambda i: (i, 0))] * 2,
        out_shape=[jax.ShapeDtypeStruct((t, d), F32), jax.ShapeDtypeStruct((t, d), BF16)],
        scratch_shapes=[pltpu.VMEM((MIX_TM, w_out.shape[0]), BF16),
                        pltpu.VMEM((MIX_TM, width), BF16),
                        pltpu.VMEM((groups, SUPER, SUPER), BF16)],
        compiler_params=_params("arbitrary"),
        name="mix_b",
    )(z, z, w_s, bs_perm, vn_g, vn_b, x, w_in, kv, w_out, ln_g, ln_b)


def _ffn_kernel(x_hbm, xb_ref, w1_ref, w2_ref, g_ref, b_ref, o_ref, xres_ref, sem, *, side_work):
    i, j = pl.program_id(0), pl.program_id(1)
    tm = o_ref.shape[0]

    def residual_copy():
        return pltpu.make_async_copy(x_hbm.at[pl.ds(pl.multiple_of(i * tm, tm), tm), :], xres_ref, sem)

    @pl.when(j == 0)
    def _():
        residual_copy().start()
        o_ref[...] = jnp.zeros_like(o_ref)

    side_work()
    h = jnp.maximum(jnp.dot(xb_ref[...], w1_ref[...], preferred_element_type=F32), 0.0)
    o_ref[...] += jnp.dot((h * h).astype(BF16), w2_ref[...], preferred_element_type=F32)

    @pl.when(j == pl.num_programs(1) - 1)
    def _():
        residual_copy().wait()
        o_ref[...] = _layer_norm(ALPHA * xres_ref[...] + o_ref[...], g_ref[0], b_ref[0])


def _ffn(x, xb, w1, w2, ln_g, ln_b, layer, cast_jobs=()):
    t, d = x.shape
    d_ff = w1.shape[1]
    n_j = d_ff // FFN_TF
    c_in, c_out, c_shape = _cast_specs(cast_jobs, lambda i, j: i * n_j + j)
    out, *casts = pl.pallas_call(
        _with_casts(_ffn_kernel, 6, 1, len(cast_jobs)),
        grid=(t // FFN_TM, n_j),
        in_specs=[pl.BlockSpec(memory_space=pl.ANY),
                  pl.BlockSpec((FFN_TM, d), lambda i, j: (i, 0)),
                  pl.BlockSpec((d, FFN_TF), lambda i, j: (0, j)),
                  pl.BlockSpec((FFN_TF, d), lambda i, j: (j, 0)),
                  pl.BlockSpec((1, 1, d), lambda i, j: (layer, 0, 0)),
                  pl.BlockSpec((1, 1, d), lambda i, j: (layer, 0, 0))] + c_in,
        out_specs=[pl.BlockSpec((FFN_TM, d), lambda i, j: (i, 0))] + c_out,
        out_shape=[jax.ShapeDtypeStruct((t, d), F32)] + c_shape,
        scratch_shapes=[pltpu.VMEM((FFN_TM, d), F32), pltpu.SemaphoreType.DMA(())],
        compiler_params=_params("arbitrary", "arbitrary"),
        name="ffn",
    )(x, xb, w1, w2, ln_g, ln_b, *[job[0] for job in cast_jobs])
    return out, casts


def _gmlp_bias_in_working_order(b_s):
    n_layers, groups, chunk = b_s.shape
    half = chunk // RUN
    reps = SUPER // chunk
    b = b_s.reshape(n_layers, groups, half, RUN).transpose(0, 3, 2, 1)
    b = jnp.broadcast_to(b[:, :, None], (n_layers, RUN, reps, half, groups))
    return b.reshape(n_layers, SUPER, groups)


def kernel(x, mem, w_in_a, w_in_b, w_s, b_s, vnorm_g, vnorm_b, w_mem_kv, w_out, ln1_g, ln1_b, w_ff1, w_ff2,
           ln2_g, ln2_b):
    batch, seq, d = x.shape
    row3 = lambda a: a.reshape(a.shape[0], 1, a.shape[1])
    ln1_g, ln1_b, ln2_g, ln2_b = row3(ln1_g), row3(ln1_b), row3(ln2_g), row3(ln2_b)
    vnorm_g, vnorm_b = row3(vnorm_g), row3(vnorm_b)
    bs_perm = _gmlp_bias_in_working_order(b_s)
    a_cols = w_in_a.shape[2] - MEM_WIDTH
    b_cols = w_in_b.shape[2] - MEM_WIDTH

    def w_in_of(layer):
        return (w_in_a, layer // 2) if layer % 2 == 0 else (w_in_b, layer // 2)

    xf = _perm(x.reshape(batch * seq, d))
    kv = _memkv(mem, w_mem_kv)
    w_in = w_in_a[0].astype(BF16)
    w_o = w_1 = w_2 = None
    for i in range(DEPTH):
        if i % 2 == 0:
            jobs = [(w_out, 0, 48), (w_ff1, 0, 64), (w_ff2, 0, 64)] if i == 0 else []
            qkv, casts = _proj(xf, w_in, a_cols, gelu=False, cast_jobs=jobs)
            if i == 0:
                w_o, w_1, w_2 = casts
            outs, lses = [], []
            for g, (window, dilation) in enumerate(A_PAIRS):
                o, lse = _attn_group(qkv, batch, seq, g, window, dilation)
                outs.append(o)
                lses.append(lse)
            xf, xb = _mix_a(xf, outs, lses, seq, w_in, kv, w_o, ln1_g, ln1_b, i)
        else:
            z, _ = _proj(xf, w_in, b_cols, gelu=True)
            xf, xb = _mix_b(xf, z, seq, w_s, bs_perm, vnorm_g, vnorm_b, i // 2, w_in, kv, w_o, ln1_g, ln1_b, i)
        if i + 1 < DEPTH:
            nxt_in, nxt_layer = w_in_of(i + 1)
            jobs = [(nxt_in, nxt_layer, 128), (w_out, i + 1, 96), (w_ff1, i + 1, 128), (w_ff2, i + 1, 128)]
            xf, (w_in, w_o, w_1, w_2) = _ffn(xf, xb, w_1, w_2, ln2_g, ln2_b, i, cast_jobs=jobs)
        else:
            xf, _ = _ffn(xf, xb, w_1, w_2, ln2_g, ln2_b, i)
    return _perm(xf).reshape(batch, seq, d)
```

```python
import functools

import numpy as np
import jax
import jax.numpy as jnp
from jax import lax
from jax.experimental import pallas as pl
from jax.experimental.pallas import tpu as pltpu

F32 = jnp.float32
BF16 = jnp.bfloat16

HEAD_DIM = 128
A_PAIRS = ((128, 1), (512, 4), (2048, 16))
A_GROUPS = len(A_PAIRS)
A_WINDOW = 128
B_CHUNK = 128
MEM_HEADS = 4
MEM_WIDTH = MEM_HEADS * HEAD_DIM
DEPTH = 4
ALPHA = (2.0 * DEPTH) ** 0.25
LN_EPS = 1e-5
SCALE = HEAD_DIM ** -0.5
SQRT_HALF = np.float32(np.sqrt(0.5))

LANES = 128
VMEM_LIMIT_BYTES = 56 * 1024 * 1024

RUN = 16
SUPER = RUN * RUN

PERM_ROWS = 4 * SUPER
PROJ_TM, PROJ_TN = 1024, 1024
ATTN_ROWS = 512
MIX_TM = 512
MIX_SUB = MIX_TM
FFN_TM, FFN_TF = 1024, 512

_NT = (((1,), (1,)), ((), ()))


def _params(*sem):
    return pltpu.CompilerParams(dimension_semantics=sem, vmem_limit_bytes=VMEM_LIMIT_BYTES)


def _cast_specs(jobs, step_of):
    in_specs, out_specs, out_shapes = [], [], []
    for src, layer, n_blocks in jobs:
        _, rows, cols = src.shape
        assert rows % n_blocks == 0 and (rows // n_blocks) % RUN == 0
        blk = rows // n_blocks
        in_specs.append(pl.BlockSpec(
            (1, blk, cols), lambda *ids, _l=layer, _n=n_blocks: (_l, jnp.minimum(step_of(*ids), _n - 1), 0)))
        out_specs.append(pl.BlockSpec(
            (blk, cols), lambda *ids, _n=n_blocks: (jnp.minimum(step_of(*ids), _n - 1), 0)))
        out_shapes.append(jax.ShapeDtypeStruct((rows, cols), BF16))
    return in_specs, out_specs, out_shapes


def _with_casts(body, n_in, n_out, n_cast):
    def kernel(*refs):
        ins, refs = refs[:n_in], refs[n_in:]
        cast_src, refs = refs[:n_cast], refs[n_cast:]
        outs, refs = refs[:n_out], refs[n_out:]
        cast_dst, scratch = refs[:n_cast], refs[n_cast:]

        def side_work():
            for src, dst in zip(cast_src, cast_dst):
                dst[...] = src[0].astype(BF16)

        body(*ins, *outs, *scratch, side_work=side_work)
    return kernel


def _layer_norm(x, g, b):
    mu = jnp.mean(x, axis=-1, keepdims=True)
    xc = x - mu
    var = jnp.mean(xc * xc, axis=-1, keepdims=True)
    return xc * lax.rsqrt(var + LN_EPS) * g + b


def _perm_kernel(x_ref, o_ref, slab_ref):
    n_slabs, rows, _ = slab_ref.shape
    for c in range(n_slabs):
        slab_ref[c] = x_ref[:, c * LANES:(c + 1) * LANES]

    def one_superblock(sb, carry):
        base = pl.multiple_of(sb * SUPER, SUPER)
        for r in range(RUN):
            for c in range(n_slabs):
                o_ref[pl.ds(base + r * RUN, RUN), c * LANES:(c + 1) * LANES] = (
                    slab_ref[c, pl.ds(base + r, RUN, stride=RUN), :])
        return carry

    lax.fori_loop(0, rows // SUPER, one_superblock, 0)


def _perm(x):
    t, d = x.shape
    return pl.pallas_call(
        _perm_kernel,
        grid=(t // PERM_ROWS,),
        in_specs=[pl.BlockSpec((PERM_ROWS, d), lambda i: (i, 0))],
        out_specs=pl.BlockSpec((PERM_ROWS, d), lambda i: (i, 0)),
        out_shape=jax.ShapeDtypeStruct((t, d), x.dtype),
        scratch_shapes=[pltpu.VMEM((d // LANES, PERM_ROWS, LANES), x.dtype)],
        compiler_params=_params("parallel"),
        name="perm",
    )(x)


def _memkv_kernel(mem_ref, w_ref, o_ref):
    o_ref[0, 0] = jnp.dot(mem_ref[0].astype(BF16), w_ref[0].astype(BF16),
                          preferred_element_type=F32).astype(BF16)


def _memkv(mem, w_mem_kv):
    nb, m, d = mem.shape
    depth, _, n = w_mem_kv.shape
    return pl.pallas_call(
        _memkv_kernel,
        grid=(depth, nb),
        in_specs=[pl.BlockSpec((1, m, d), lambda i, b: (b, 0, 0)),
                  pl.BlockSpec((1, d, n), lambda i, b: (i, 0, 0))],
        out_specs=pl.BlockSpec((1, 1, m, n), lambda i, b: (i, b, 0, 0)),
        out_shape=jax.ShapeDtypeStruct((depth, nb, m, n), BF16),
        compiler_params=_params("parallel", "arbitrary"),
        name="memkv",
    )(mem, w_mem_kv)


def _proj_kernel(x_ref, w_ref, o_ref, xb_ref, *, gelu, side_work):
    @pl.when(pl.program_id(1) == 0)
    def _():
        xb_ref[...] = x_ref[...].astype(BF16)

    side_work()
    acc = jnp.dot(xb_ref[...], w_ref[...], preferred_element_type=F32)
    if gelu:
        acc = 0.5 * acc * (1.0 + lax.erf(acc * SQRT_HALF))
    o_ref[...] = acc.astype(o_ref.dtype)


def _proj(x, w, n_out, gelu, cast_jobs=()):
    t, d = x.shape
    n_j = n_out // PROJ_TN
    c_in, c_out, c_shape = _cast_specs(cast_jobs, lambda i, j: i * n_j + j)
    out, *casts = pl.pallas_call(
        _with_casts(functools.partial(_proj_kernel, gelu=gelu), 2, 1, len(cast_jobs)),
        grid=(t // PROJ_TM, n_j),
        in_specs=[pl.BlockSpec((PROJ_TM, d), lambda i, j: (i, 0)),
                  pl.BlockSpec((d, PROJ_TN), lambda i, j: (0, j))] + c_in,
        out_specs=[pl.BlockSpec((PROJ_TM, PROJ_TN), lambda i, j: (i, j))] + c_out,
        out_shape=[jax.ShapeDtypeStruct((t, n_out), BF16)] + c_shape,
        scratch_shapes=[pltpu.VMEM((PROJ_TM, d), BF16)],
        compiler_params=_params("arbitrary", "arbitrary"),
        name="proj_gelu" if gelu else "proj",
    )(x, w, *[job[0] for job in cast_jobs])
    return out, casts


def _class_pos(i, dilation):
    per_super = SUPER // dilation
    n_a = RUN // dilation
    return (i & ~(per_super - 1)) + n_a * (i & (RUN - 1)) + ((i >> 4) & (n_a - 1))


def _attn_kernel(q_ref, kc_ref, kp_ref, vc_ref, vp_ref, o_ref, lse_ref,
                 qbuf, kbuf, vbuf, obuf, lbuf, bias_ref, *, rows, unit, tile, heads, dilation):
    n_keys = 2 * unit
    width = heads * HEAD_DIM
    chunk = pl.program_id(2)
    first_step = (pl.program_id(0) == 0) & (pl.program_id(1) == 0) & (chunk == 0)

    @pl.when(first_step)
    def _():
        neg_inf = jnp.float32(-jnp.inf)
        for qt in range(unit // tile):
            row = lax.broadcasted_iota(jnp.int32, (tile, n_keys), 0) + qt * tile
            col = lax.broadcasted_iota(jnp.int32, (tile, n_keys), 1)
            dist = unit + _class_pos(row, dilation) - (_class_pos(col & (unit - 1), dilation) + (col & unit))
            band = (dist >= 0) & (dist <= A_WINDOW)
            bias_ref[0, qt] = jnp.where(band & (col >= unit), 0.0, neg_inf)
            bias_ref[1, qt] = jnp.where(band, 0.0, neg_inf)

    qbuf[...] = q_ref[...].reshape(rows, width)
    kbuf[0:unit, :] = kp_ref[...].reshape(unit, width)
    kbuf[unit:, :] = kc_ref[...].reshape(rows, width)
    vbuf[0:unit, :] = vp_ref[...].reshape(unit, width)
    vbuf[unit:, :] = vc_ref[...].reshape(rows, width)

    has_prev = jnp.where(chunk > 0, 1, 0)
    lane = lax.broadcasted_iota(jnp.int32, (tile, LANES), 1)
    for u in range(rows // unit):
        k0 = u * unit
        for qt in range(unit // tile):
            bias = bias_ref[has_prev if u == 0 else 1, qt]
            r0 = k0 + qt * tile
            lse_tile = jnp.zeros((tile, LANES), F32)
            for h in range(heads):
                cs = slice(h * HEAD_DIM, (h + 1) * HEAD_DIM)
                s = lax.dot_general(qbuf[r0:r0 + tile, cs], kbuf[k0:k0 + n_keys, cs], _NT,
                                    preferred_element_type=F32) * SCALE + bias
                m = jnp.max(s, axis=-1, keepdims=True)
                p = jnp.exp(s - m)
                l = jnp.sum(p, axis=-1, keepdims=True)
                o = jnp.dot(p.astype(BF16), vbuf[k0:k0 + n_keys, cs], preferred_element_type=F32) / l
                obuf[r0:r0 + tile, cs] = o.astype(BF16)
                lse_tile = jnp.where(lane == h, m + jnp.log(l), lse_tile)
            lbuf[r0:r0 + tile, :] = lse_tile
    o_ref[...] = obuf[...].reshape(o_ref.shape)
    lse_ref[...] = lbuf[...].reshape(lse_ref.shape)


def _attn_group(qkv, batch, seq, group, window, dilation):
    t, cols = qkv.shape
    width = cols // (A_GROUPS * 3)
    heads = width // HEAD_DIM
    assert window // dilation == A_WINDOW and RUN % dilation == 0
    n_super = seq // SUPER
    n_a = RUN // dilation
    per_super = SUPER // dilation
    unit = max(A_WINDOW, per_super)
    rows = min(ATTN_ROWS, seq // dilation)
    assert rows % unit == 0 and unit % per_super == 0
    tile = unit

    def view(a, last):
        return a.reshape(batch, n_super, n_a, dilation, RUN, last)

    def cur(last, col_block):
        return pl.BlockSpec((None, rows // per_super, n_a, None, RUN, last),
                            lambda b, r, c: (b, c, 0, r, 0, col_block))

    def prev(col_block):
        return pl.BlockSpec((None, unit // per_super, n_a, None, RUN, width),
                            lambda b, r, c: (b, jnp.maximum(c * (rows // unit) - 1, 0), 0, r, 0, col_block))

    qkv_v = view(qkv, cols)
    q_col, k_col, v_col = (group * 3 + i for i in range(3))
    o, lse = pl.pallas_call(
        functools.partial(_attn_kernel, rows=rows, unit=unit, tile=tile, heads=heads, dilation=dilation),
        grid=(batch, dilation, (seq // dilation) // rows),
        in_specs=[cur(width, q_col), cur(width, k_col), prev(k_col), cur(width, v_col), prev(v_col)],
        out_specs=[cur(width, 0), cur(LANES, 0)],
        out_shape=[jax.ShapeDtypeStruct((batch, n_super, n_a, dilation, RUN, width), BF16),
                   jax.ShapeDtypeStruct((batch, n_super, n_a, dilation, RUN, LANES), F32)],
        scratch_shapes=[pltpu.VMEM((rows, width), BF16),
                        pltpu.VMEM((rows + unit, width), BF16),
                        pltpu.VMEM((rows + unit, width), BF16),
                        pltpu.VMEM((rows, width), BF16),
                        pltpu.VMEM((rows, LANES), F32),
                        pltpu.VMEM((2, unit // tile, tile, 2 * unit), F32)],
        compiler_params=_params("arbitrary", "arbitrary", "arbitrary"),
        name=f"attn_d{dilation}",
    )(qkv_v, qkv_v, qkv_v, qkv_v, qkv_v)
    return o.reshape(t, width), lse.reshape(t, LANES)


def _mem_attn_outproj_ln(rs, x_ref, wmq_ref, kv_ref, wout_ref, g_ref, b_ref, out_ref, outb_ref, cat_ref,
                         mix_width):
    xf = x_ref[rs, :]
    y = jnp.dot(cat_ref[rs, 0:mix_width], wout_ref[0:mix_width, :], preferred_element_type=F32)
    mq = jnp.dot(xf.astype(BF16), wmq_ref[...], preferred_element_type=F32).astype(BF16)
    for h in range(MEM_HEADS):
        c0 = h * HEAD_DIM
        k = kv_ref[0, 0, :, c0:c0 + HEAD_DIM]
        v = kv_ref[0, 0, :, MEM_WIDTH + c0:MEM_WIDTH + c0 + HEAD_DIM]
        s = lax.dot_general(mq[:, c0:c0 + HEAD_DIM], k, _NT, preferred_element_type=F32) * SCALE
        p = jnp.exp(s - jnp.max(s, axis=-1, keepdims=True))
        l = jnp.sum(p, axis=-1, keepdims=True)
        o = jnp.dot(p.astype(BF16), v, preferred_element_type=F32) / l
        cat_ref[rs, mix_width + c0:mix_width + c0 + HEAD_DIM] = o.astype(BF16)
    y = y + jnp.dot(cat_ref[rs, mix_width:], wout_ref[mix_width:, :], preferred_element_type=F32)
    x1 = _layer_norm(ALPHA * xf + y, g_ref[0], b_ref[0])
    out_ref[rs, :] = x1
    outb_ref[rs, :] = x1.astype(BF16)


def _row_blocks(tm):
    return [slice(r, r + MIX_SUB) for r in range(0, tm, MIX_SUB)]


def _mix_a_kernel(o0_ref, o1_ref, o2_ref, l0_ref, l1_ref, l2_ref, x_ref, wmq_ref, kv_ref, wout_ref,
                  g_ref, b_ref, out_ref, outb_ref, cat_ref, *, heads):
    for rs in _row_blocks(x_ref.shape[0]):
        l0, l1, l2 = l0_ref[rs, :], l1_ref[rs, :], l2_ref[rs, :]
        mx = jnp.maximum(jnp.maximum(l0, l1), l2)
        e0, e1, e2 = jnp.exp(l0 - mx), jnp.exp(l1 - mx), jnp.exp(l2 - mx)
        den = e0 + e1 + e2
        w0, w1 = e0 / den, e1 / den
        for h in range(heads):
            sl = slice(h * HEAD_DIM, (h + 1) * HEAD_DIM)
            o2 = o2_ref[rs, sl].astype(F32)
            mix = (o2 + w0[:, h:h + 1] * (o0_ref[rs, sl].astype(F32) - o2)
                   + w1[:, h:h + 1] * (o1_ref[rs, sl].astype(F32) - o2))
            cat_ref[rs, sl] = mix.astype(BF16)
        _mem_attn_outproj_ln(rs, x_ref, wmq_ref, kv_ref, wout_ref, g_ref, b_ref, out_ref, outb_ref, cat_ref,
                             heads * HEAD_DIM)


def _mix_b_kernel(u_ref, v_ref, ws_ref, bs_ref, vg_ref, vb_ref, x_ref, wmq_ref, kv_ref, wout_ref,
                  g_ref, b_ref, out_ref, outb_ref, cat_ref, vn_ref, wsp_ref, *, groups):
    tm = u_ref.shape[0]

    @pl.when(pl.program_id(0) == 0)
    def _():
        row = lax.broadcasted_iota(jnp.int32, (SUPER, SUPER), 0)
        col = lax.broadcasted_iota(jnp.int32, (SUPER, SUPER), 1)
        nat_row = RUN * (row & (RUN - 1)) + (row >> 4)
        nat_col = RUN * (col & (RUN - 1)) + (col >> 4)
        same_chunk = (nat_row // B_CHUNK) == (nat_col // B_CHUNK)
        causal = (same_chunk & (nat_col <= nat_row)).astype(F32)
        i = lax.broadcasted_iota(jnp.int32, (SUPER, B_CHUNK), 0)
        p = lax.broadcasted_iota(jnp.int32, (SUPER, B_CHUNK), 1)
        pos = (RUN * (i & (RUN - 1)) + (i >> 4)) % B_CHUNK
        pick = (p == pos).astype(BF16)
        for g in range(groups):
            rows = jnp.dot(pick, ws_ref[0, g].astype(BF16), preferred_element_type=F32).astype(BF16)
            full = lax.dot_general(rows, pick, _NT, preferred_element_type=F32)
            wsp_ref[g] = (full * causal).astype(BF16)

    for rs in _row_blocks(tm):
        vn_ref[rs, :] = _layer_norm(v_ref[rs, :].astype(F32), vg_ref[0], vb_ref[0]).astype(BF16)
        for g in range(groups):
            sl = slice(g * HEAD_DIM, (g + 1) * HEAD_DIM)
            for s0 in range(rs.start, rs.stop, SUPER):
                sb = slice(s0, s0 + SUPER)
                sg = jnp.dot(wsp_ref[g], vn_ref[sb, sl], preferred_element_type=F32) + bs_ref[0, :, g:g + 1]
                cat_ref[sb, sl] = (u_ref[sb, sl].astype(F32) * sg).astype(BF16)
        _mem_attn_outproj_ln(rs, x_ref, wmq_ref, kv_ref, wout_ref, g_ref, b_ref, out_ref, outb_ref, cat_ref,
                             groups * HEAD_DIM)


def _mix_tail_in_specs(d, seq, w_in_shape, layer, kv_shape, wout_shape):
    tiles_per_batch = seq // MIX_TM
    _, _, m, kvw = kv_shape
    k_out, _ = wout_shape
    mq_block = (w_in_shape[1] - MEM_WIDTH) // MEM_WIDTH
    return [
        pl.BlockSpec((MIX_TM, d), lambda i: (i, 0)),
        pl.BlockSpec((d, MEM_WIDTH), lambda i: (0, mq_block)),
        pl.BlockSpec((1, 1, m, kvw), lambda i: (layer, i // tiles_per_batch, 0, 0)),
        pl.BlockSpec((k_out, d), lambda i: (0, 0)),
        pl.BlockSpec((1, 1, d), lambda i: (layer, 0, 0)),
        pl.BlockSpec((1, 1, d), lambda i: (layer, 0, 0)),
    ]


def _mix_a(x, outs, lses, seq, w_in, kv, w_out, ln_g, ln_b, layer):
    t, d = x.shape
    width = outs[0].shape[1]
    tail = _mix_tail_in_specs(d, seq, w_in.shape, layer, kv.shape, w_out.shape)
    return pl.pallas_call(
        functools.partial(_mix_a_kernel, heads=width // HEAD_DIM),
        grid=(t // MIX_TM,),
        in_specs=[pl.BlockSpec((MIX_TM, width), lambda i: (i, 0))] * 3
                 + [pl.BlockSpec((MIX_TM, LANES), lambda i: (i, 0))] * 3 + tail,
        out_specs=[pl.BlockSpec((MIX_TM, d), lambda i: (i, 0))] * 2,
        out_shape=[jax.ShapeDtypeStruct((t, d), F32), jax.ShapeDtypeStruct((t, d), BF16)],
        scratch_shapes=[pltpu.VMEM((MIX_TM, w_out.shape[0]), BF16)],
        compiler_params=_params("parallel"),
        name="mix_a",
    )(*outs, *lses, x, w_in, kv, w_out, ln_g, ln_b)


def _mix_b(x, z, seq, w_s, bs_perm, vn_g, vn_b, layer_in, w_in, kv, w_out, ln_g, ln_b, layer):
    t, d = x.shape
    width = z.shape[1] // 2
    groups = width // HEAD_DIM
    tail = _mix_tail_in_specs(d, seq, w_in.shape, layer, kv.shape, w_out.shape)
    return pl.pallas_call(
        functools.partial(_mix_b_kernel, groups=groups),
        grid=(t // MIX_TM,),
        in_specs=[pl.BlockSpec((MIX_TM, width), lambda i: (i, 0)),
                  pl.BlockSpec((MIX_TM, width), lambda i: (i, 1)),
                  pl.BlockSpec((1, groups, B_CHUNK, B_CHUNK), lambda i: (layer_in, 0, 0, 0)),
                  pl.BlockSpec((1, SUPER, groups), lambda i: (layer_in, 0, 0)),
                  pl.BlockSpec((1, 1, width), lambda i: (layer_in, 0, 0)),
                  pl.BlockSpec((1, 1, width), lambda i: (layer_in, 0, 0))] + tail,
        out_specs=[pl.BlockSpec((MIX_TM, d), lambda i: (i, 0))] * 2,
        out_shape=[jax.ShapeDtypeStruct((t, d), F32), jax.ShapeDtypeStruct((t, d), BF16)],
        scratch_shapes=[pltpu.VMEM((MIX_TM, w_out.shape[0]), BF16),
                        pltpu.VMEM((MIX_TM, width), BF16),
                        pltpu.VMEM((groups, SUPER, SUPER), BF16)],
        compiler_params=_params("arbitrary"),
        name="mix_b",
    )(z, z, w_s, bs_perm, vn_g, vn_b, x, w_in, kv, w_out, ln_g, ln_b)


def _ffn_kernel(x_hbm, xb_ref, w1_ref, w2_ref, g_ref, b_ref, o_ref, xres_ref, sem, *, side_work):
    i, j = pl.program_id(0), pl.program_id(1)
    tm = o_ref.shape[0]

    def residual_copy():
        return pltpu.make_async_copy(x_hbm.at[pl.ds(pl.multiple_of(i * tm, tm), tm), :], xres_ref, sem)

    @pl.when(j == 0)
    def _():
        residual_copy().start()
        o_ref[...] = jnp.zeros_like(o_ref)

    side_work()
    h = jnp.maximum(jnp.dot(xb_ref[...], w1_ref[...], preferred_element_type=F32), 0.0)
    o_ref[...] += jnp.dot((h * h).astype(BF16), w2_ref[...], preferred_element_type=F32)

    @pl.when(j == pl.num_programs(1) - 1)
    def _():
        residual_copy().wait()
        o_ref[...] = _layer_norm(ALPHA * xres_ref[...] + o_ref[...], g_ref[0], b_ref[0])


def _ffn(x, xb, w1, w2, ln_g, ln_b, layer, cast_jobs=()):
    t, d = x.shape
    d_ff = w1.shape[1]
    n_j = d_ff // FFN_TF
    c_in, c_out, c_shape = _cast_specs(cast_jobs, lambda i, j: i * n_j + j)
    out, *casts = pl.pallas_call(
        _with_casts(_ffn_kernel, 6, 1, len(cast_jobs)),
        grid=(t // FFN_TM, n_j),
        in_specs=[pl.BlockSpec(memory_space=pl.ANY),
                  pl.BlockSpec((FFN_TM, d), lambda i, j: (i, 0)),
                  pl.BlockSpec((d, FFN_TF), lambda i, j: (0, j)),
                  pl.BlockSpec((FFN_TF, d), lambda i, j: (j, 0)),
                  pl.BlockSpec((1, 1, d), lambda i, j: (layer, 0, 0)),
                  pl.BlockSpec((1, 1, d), lambda i, j: (layer, 0, 0))] + c_in,
        out_specs=[pl.BlockSpec((FFN_TM, d), lambda i, j: (i, 0))] + c_out,
        out_shape=[jax.ShapeDtypeStruct((t, d), F32)] + c_shape,
        scratch_shapes=[pltpu.VMEM((FFN_TM, d), F32), pltpu.SemaphoreType.DMA(())],
        compiler_params=_params("arbitrary", "arbitrary"),
        name="ffn",
    )(x, xb, w1, w2, ln_g, ln_b, *[job[0] for job in cast_jobs])
    return out, casts


def _gmlp_bias_in_working_order(b_s):
    n_layers, groups, chunk = b_s.shape
    half = chunk // RUN
    reps = SUPER // chunk
    b = b_s.reshape(n_layers, groups, half, RUN).transpose(0, 3, 2, 1)
    b = jnp.broadcast_to(b[:, :, None], (n_layers, RUN, reps, half, groups))
    return b.reshape(n_layers, SUPER, groups)


def kernel(x, mem, w_in_a, w_in_b, w_s, b_s, vnorm_g, vnorm_b, w_mem_kv, w_out, ln1_g, ln1_b, w_ff1, w_ff2,
           ln2_g, ln2_b):
    batch, seq, d = x.shape
    row3 = lambda a: a.reshape(a.shape[0], 1, a.shape[1])
    ln1_g, ln1_b, ln2_g, ln2_b = row3(ln1_g), row3(ln1_b), row3(ln2_g), row3(ln2_b)
    vnorm_g, vnorm_b = row3(vnorm_g), row3(vnorm_b)
    bs_perm = _gmlp_bias_in_working_order(b_s)
    a_cols = w_in_a.shape[2] - MEM_WIDTH
    b_cols = w_in_b.shape[2] - MEM_WIDTH

    def w_in_of(layer):
        return (w_in_a, layer // 2) if layer % 2 == 0 else (w_in_b, layer // 2)

    xf = _perm(x.reshape(batch * seq, d))
    kv = _memkv(mem, w_mem_kv)
    w_in = w_in_a[0].astype(BF16)
    w_o = w_1 = w_2 = None
    for i in range(DEPTH):
        if i % 2 == 0:
            jobs = [(w_out, 0, 48), (w_ff1, 0, 64), (w_ff2, 0, 64)] if i == 0 else []
            qkv, casts = _proj(xf, w_in, a_cols, gelu=False, cast_jobs=jobs)
            if i == 0:
                w_o, w_1, w_2 = casts
            outs, lses = [], []
            for g, (window, dilation) in enumerate(A_PAIRS):
                o, lse = _attn_group(qkv, batch, seq, g, window, dilation)
                outs.append(o)
                lses.append(lse)
            xf, xb = _mix_a(xf, outs, lses, seq, w_in, kv, w_o, ln1_g, ln1_b, i)
        else:
            z, _ = _proj(xf, w_in, b_cols, gelu=True)
            xf, xb = _mix_b(xf, z, seq, w_s, bs_perm, vnorm_g, vnorm_b, i // 2, w_in, kv, w_o, ln1_g, ln1_b, i)
        if i + 1 < DEPTH:
            nxt_in, nxt_layer = w_in_of(i + 1)
            jobs = [(nxt_in, nxt_layer, 128), (w_out, i + 1, 96), (w_ff1, i + 1, 128), (w_ff2, i + 1, 128)]
            xf, (w_in, w_o, w_1, w_2) = _ffn(xf, xb, w_1, w_2, ln2_g, ln2_b, i, cast_jobs=jobs)
        else:
            xf, _ = _ffn(xf, xb, w_1, w_2, ln2_g, ln2_b, i)
    return _perm(xf).reshape(batch, seq, d)
```

```python
import functools

import numpy as np
import jax
import jax.numpy as jnp
from jax import lax
from jax.experimental import pallas as pl
from jax.experimental.pallas import tpu as pltpu

F32 = jnp.float32
BF16 = jnp.bfloat16

HEAD_DIM = 128
A_PAIRS = ((128, 1), (512, 4), (2048, 16))
A_GROUPS = len(A_PAIRS)
A_WINDOW = 128
B_CHUNK = 128
MEM_HEADS = 4
MEM_WIDTH = MEM_HEADS * HEAD_DIM
DEPTH = 4
ALPHA = (2.0 * DEPTH) ** 0.25
LN_EPS = 1e-5
SCALE = HEAD_DIM ** -0.5
SQRT_HALF = np.float32(np.sqrt(0.5))

LANES = 128
VMEM_LIMIT_BYTES = 56 * 1024 * 1024

RUN = 16
SUPER = RUN * RUN

PERM_ROWS = 4 * SUPER
PROJ_TM, PROJ_TN = 1024, 1024
ATTN_ROWS = 1024
MIX_TM = 512
MIX_SUB = MIX_TM
FFN_TM, FFN_TF = 1024, 512

_NT = (((1,), (1,)), ((), ()))


def _params(*sem):
    return pltpu.CompilerParams(dimension_semantics=sem, vmem_limit_bytes=VMEM_LIMIT_BYTES)


def _cast_specs(jobs, step_of):
    in_specs, out_specs, out_shapes = [], [], []
    for src, layer, n_blocks in jobs:
        _, rows, cols = src.shape
        assert rows % n_blocks == 0 and (rows // n_blocks) % RUN == 0
        blk = rows // n_blocks
        in_specs.append(pl.BlockSpec(
            (1, blk, cols), lambda *ids, _l=layer, _n=n_blocks: (_l, jnp.minimum(step_of(*ids), _n - 1), 0)))
        out_specs.append(pl.BlockSpec(
            (blk, cols), lambda *ids, _n=n_blocks: (jnp.minimum(step_of(*ids), _n - 1), 0)))
        out_shapes.append(jax.ShapeDtypeStruct((rows, cols), BF16))
    return in_specs, out_specs, out_shapes


def _with_casts(body, n_in, n_out, n_cast):
    def kernel(*refs):
        ins, refs = refs[:n_in], refs[n_in:]
        cast_src, refs = refs[:n_cast], refs[n_cast:]
        outs, refs = refs[:n_out], refs[n_out:]
        cast_dst, scratch = refs[:n_cast], refs[n_cast:]

        def side_work():
            for src, dst in zip(cast_src, cast_dst):
                dst[...] = src[0].astype(BF16)

        body(*ins, *outs, *scratch, side_work=side_work)
    return kernel


def _layer_norm(x, g, b):
    mu = jnp.mean(x, axis=-1, keepdims=True)
    xc = x - mu
    var = jnp.mean(xc * xc, axis=-1, keepdims=True)
    return xc * lax.rsqrt(var + LN_EPS) * g + b


def _perm_kernel(x_ref, o_ref, slab_ref):
    n_slabs, rows, _ = slab_ref.shape
    for c in range(n_slabs):
        slab_ref[c] = x_ref[:, c * LANES:(c + 1) * LANES]

    def one_superblock(sb, carry):
        base = pl.multiple_of(sb * SUPER, SUPER)
        for r in range(RUN):
            for c in range(n_slabs):
                o_ref[pl.ds(base + r * RUN, RUN), c * LANES:(c + 1) * LANES] = (
                    slab_ref[c, pl.ds(base + r, RUN, stride=RUN), :])
        return carry

    lax.fori_loop(0, rows // SUPER, one_superblock, 0)


def _perm(x):
    t, d = x.shape
    return pl.pallas_call(
        _perm_kernel,
        grid=(t // PERM_ROWS,),
        in_specs=[pl.BlockSpec((PERM_ROWS, d), lambda i: (i, 0))],
        out_specs=pl.BlockSpec((PERM_ROWS, d), lambda i: (i, 0)),
        out_shape=jax.ShapeDtypeStruct((t, d), x.dtype),
        scratch_shapes=[pltpu.VMEM((d // LANES, PERM_ROWS, LANES), x.dtype)],
        compiler_params=_params("parallel"),
        name="perm",
    )(x)


def _memkv_kernel(mem_ref, w_ref, o_ref):
    o_ref[0, 0] = jnp.dot(mem_ref[0].astype(BF16), w_ref[0].astype(BF16),
                          preferred_element_type=F32).astype(BF16)


def _memkv(mem, w_mem_kv):
    nb, m, d = mem.shape
    depth, _, n = w_mem_kv.shape
    return pl.pallas_call(
        _memkv_kernel,
        grid=(depth, nb),
        in_specs=[pl.BlockSpec((1, m, d), lambda i, b: (b, 0, 0)),
                  pl.BlockSpec((1, d, n), lambda i, b: (i, 0, 0))],
        out_specs=pl.BlockSpec((1, 1, m, n), lambda i, b: (i, b, 0, 0)),
        out_shape=jax.ShapeDtypeStruct((depth, nb, m, n), BF16),
        compiler_params=_params("parallel", "arbitrary"),
        name="memkv",
    )(mem, w_mem_kv)


def _proj_kernel(x_ref, w_ref, o_ref, xb_ref, *, gelu, side_work):
    @pl.when(pl.program_id(1) == 0)
    def _():
        xb_ref[...] = x_ref[...].astype(BF16)

    side_work()
    acc = jnp.dot(xb_ref[...], w_ref[...], preferred_element_type=F32)
    if gelu:
        acc = 0.5 * acc * (1.0 + lax.erf(acc * SQRT_HALF))
    o_ref[...] = acc.astype(o_ref.dtype)


def _proj(x, w, n_out, gelu, cast_jobs=()):
    t, d = x.shape
    n_j = n_out // PROJ_TN
    c_in, c_out, c_shape = _cast_specs(cast_jobs, lambda i, j: i * n_j + j)
    out, *casts = pl.pallas_call(
        _with_casts(functools.partial(_proj_kernel, gelu=gelu), 2, 1, len(cast_jobs)),
        grid=(t // PROJ_TM, n_j),
        in_specs=[pl.BlockSpec((PROJ_TM, d), lambda i, j: (i, 0)),
                  pl.BlockSpec((d, PROJ_TN), lambda i, j: (0, j))] + c_in,
        out_specs=[pl.BlockSpec((PROJ_TM, PROJ_TN), lambda i, j: (i, j))] + c_out,
        out_shape=[jax.ShapeDtypeStruct((t, n_out), BF16)] + c_shape,
        scratch_shapes=[pltpu.VMEM((PROJ_TM, d), BF16)],
        compiler_params=_params("arbitrary", "arbitrary"),
        name="proj_gelu" if gelu else "proj",
    )(x, w, *[job[0] for job in cast_jobs])
    return out, casts


def _class_pos(i, dilation):
    per_super = SUPER // dilation
    n_a = RUN // dilation
    return (i & ~(per_super - 1)) + n_a * (i & (RUN - 1)) + ((i >> 4) & (n_a - 1))


def _attn_kernel(q_ref, kc_ref, kp_ref, vc_ref, vp_ref, o_ref, lse_ref,
                 qbuf, kbuf, vbuf, obuf, lbuf, bias_ref, *, rows, unit, tile, heads, dilation):
    n_keys = 2 * unit
    width = heads * HEAD_DIM
    chunk = pl.program_id(2)
    first_step = (pl.program_id(0) == 0) & (pl.program_id(1) == 0) & (chunk == 0)

    @pl.when(first_step)
    def _():
        neg_inf = jnp.float32(-jnp.inf)
        for qt in range(unit // tile):
            row = lax.broadcasted_iota(jnp.int32, (tile, n_keys), 0) + qt * tile
            col = lax.broadcasted_iota(jnp.int32, (tile, n_keys), 1)
            dist = unit + _class_pos(row, dilation) - (_class_pos(col & (unit - 1), dilation) + (col & unit))
            band = (dist >= 0) & (dist <= A_WINDOW)
            bias_ref[0, qt] = jnp.where(band & (col >= unit), 0.0, neg_inf)
            bias_ref[1, qt] = jnp.where(band, 0.0, neg_inf)

    qbuf[...] = q_ref[...].reshape(rows, width)
    kbuf[0:unit, :] = kp_ref[...].reshape(unit, width)
    kbuf[unit:, :] = kc_ref[...].reshape(rows, width)
    vbuf[0:unit, :] = vp_ref[...].reshape(unit, width)
    vbuf[unit:, :] = vc_ref[...].reshape(rows, width)

    has_prev = jnp.where(chunk > 0, 1, 0)
    lane = lax.broadcasted_iota(jnp.int32, (tile, LANES), 1)
    for u in range(rows // unit):
        k0 = u * unit
        for qt in range(unit // tile):
            bias = bias_ref[has_prev if u == 0 else 1, qt]
            r0 = k0 + qt * tile
            lse_tile = jnp.zeros((tile, LANES), F32)
            for h in range(heads):
                cs = slice(h * HEAD_DIM, (h + 1) * HEAD_DIM)
                s = lax.dot_general(qbuf[r0:r0 + tile, cs], kbuf[k0:k0 + n_keys, cs], _NT,
                                    preferred_element_type=F32) * SCALE + bias
                m = jnp.max(s, axis=-1, keepdims=True)
                p = jnp.exp(s - m)
                l = jnp.sum(p, axis=-1, keepdims=True)
                o = jnp.dot(p.astype(BF16), vbuf[k0:k0 + n_keys, cs], preferred_element_type=F32) / l
                obuf[r0:r0 + tile, cs] = o.astype(BF16)
                lse_tile = jnp.where(lane == h, m + jnp.log(l), lse_tile)
            lbuf[r0:r0 + tile, :] = lse_tile
    o_ref[...] = obuf[...].reshape(o_ref.shape)
    lse_ref[...] = lbuf[...].reshape(lse_ref.shape)


def _attn_group(qkv, batch, seq, group, window, dilation):
    t, cols = qkv.shape
    width = cols // (A_GROUPS * 3)
    heads = width // HEAD_DIM
    assert window // dilation == A_WINDOW and RUN % dilation == 0
    n_super = seq // SUPER
    n_a = RUN // dilation
    per_super = SUPER // dilation
    unit = max(A_WINDOW, per_super)
    rows = min(ATTN_ROWS, seq // dilation)
    assert rows % unit == 0 and unit % per_super == 0
    tile = unit

    def view(a, last):
        return a.reshape(batch, n_super, n_a, dilation, RUN, last)

    def cur(last, col_block):
        return pl.BlockSpec((None, rows // per_super, n_a, None, RUN, last),
                            lambda b, r, c: (b, c, 0, r, 0, col_block))

    def prev(col_block):
        return pl.BlockSpec((None, unit // per_super, n_a, None, RUN, width),
                            lambda b, r, c: (b, jnp.maximum(c * (rows // unit) - 1, 0), 0, r, 0, col_block))

    qkv_v = view(qkv, cols)
    q_col, k_col, v_col = (group * 3 + i for i in range(3))
    o, lse = pl.pallas_call(
        functools.partial(_attn_kernel, rows=rows, unit=unit, tile=tile, heads=heads, dilation=dilation),
        grid=(batch, dilation, (seq // dilation) // rows),
        in_specs=[cur(width, q_col), cur(width, k_col), prev(k_col), cur(width, v_col), prev(v_col)],
        out_specs=[cur(width, 0), cur(LANES, 0)],
        out_shape=[jax.ShapeDtypeStruct((batch, n_super, n_a, dilation, RUN, width), BF16),
                   jax.ShapeDtypeStruct((batch, n_super, n_a, dilation, RUN, LANES), F32)],
        scratch_shapes=[pltpu.VMEM((rows, width), BF16),
                        pltpu.VMEM((rows + unit, width), BF16),
                        pltpu.VMEM((rows + unit, width), BF16),
                        pltpu.VMEM((rows, width), BF16),
                        pltpu.VMEM((rows, LANES), F32),
                        pltpu.VMEM((2, unit // tile, tile, 2 * unit), F32)],
        compiler_params=_params("arbitrary", "arbitrary", "arbitrary"),
        name=f"attn_d{dilation}",
    )(qkv_v, qkv_v, qkv_v, qkv_v, qkv_v)
    return o.reshape(t, width), lse.reshape(t, LANES)


def _mem_attn_outproj_ln(rs, x_ref, wmq_ref, kv_ref, wout_ref, g_ref, b_ref, out_ref, outb_ref, cat_ref,
                         mix_width):
    xf = x_ref[rs, :]
    y = jnp.dot(cat_ref[rs, 0:mix_width], wout_ref[0:mix_width, :], preferred_element_type=F32)
    mq = jnp.dot(xf.astype(BF16), wmq_ref[...], preferred_element_type=F32).astype(BF16)
    for h in range(MEM_HEADS):
        c0 = h * HEAD_DIM
        k = kv_ref[0, 0, :, c0:c0 + HEAD_DIM]
        v = kv_ref[0, 0, :, MEM_WIDTH + c0:MEM_WIDTH + c0 + HEAD_DIM]
        s = lax.dot_general(mq[:, c0:c0 + HEAD_DIM], k, _NT, preferred_element_type=F32) * SCALE
        p = jnp.exp(s - jnp.max(s, axis=-1, keepdims=True))
        l = jnp.sum(p, axis=-1, keepdims=True)
        o = jnp.dot(p.astype(BF16), v, preferred_element_type=F32) / l
        cat_ref[rs, mix_width + c0:mix_width + c0 + HEAD_DIM] = o.astype(BF16)
    y = y + jnp.dot(cat_ref[rs, mix_width:], wout_ref[mix_width:, :], preferred_element_type=F32)
    x1 = _layer_norm(ALPHA * xf + y, g_ref[0], b_ref[0])
    out_ref[rs, :] = x1
    outb_ref[rs, :] = x1.astype(BF16)


def _row_blocks(tm):
    return [slice(r, r + MIX_SUB) for r in range(0, tm, MIX_SUB)]


def _mix_a_kernel(o0_ref, o1_ref, o2_ref, l0_ref, l1_ref, l2_ref, x_ref, wmq_ref, kv_ref, wout_ref,
                  g_ref, b_ref, out_ref, outb_ref, cat_ref, *, heads):
    for rs in _row_blocks(x_ref.shape[0]):
        l0, l1, l2 = l0_ref[rs, :], l1_ref[rs, :], l2_ref[rs, :]
        mx = jnp.maximum(jnp.maximum(l0, l1), l2)
        e0, e1, e2 = jnp.exp(l0 - mx), jnp.exp(l1 - mx), jnp.exp(l2 - mx)
        den = e0 + e1 + e2
        w0, w1 = e0 / den, e1 / den
        for h in range(heads):
            sl = slice(h * HEAD_DIM, (h + 1) * HEAD_DIM)
            o2 = o2_ref[rs, sl].astype(F32)
            mix = (o2 + w0[:, h:h + 1] * (o0_ref[rs, sl].astype(F32) - o2)
                   + w1[:, h:h + 1] * (o1_ref[rs, sl].astype(F32) - o2))
            cat_ref[rs, sl] = mix.astype(BF16)
        _mem_attn_outproj_ln(rs, x_ref, wmq_ref, kv_ref, wout_ref, g_ref, b_ref, out_ref, outb_ref, cat_ref,
                             heads * HEAD_DIM)


def _mix_b_kernel(u_ref, v_ref, ws_ref, bs_ref, vg_ref, vb_ref, x_ref, wmq_ref, kv_ref, wout_ref,
                  g_ref, b_ref, out_ref, outb_ref, cat_ref, vn_ref, wsp_ref, *, groups):
    tm = u_ref.shape[0]

    @pl.when(pl.program_id(0) == 0)
    def _():
        row = lax.broadcasted_iota(jnp.int32, (SUPER, SUPER), 0)
        col = lax.broadcasted_iota(jnp.int32, (SUPER, SUPER), 1)
        nat_row = RUN * (row & (RUN - 1)) + (row >> 4)
        nat_col = RUN * (col & (RUN - 1)) + (col >> 4)
        same_chunk = (nat_row // B_CHUNK) == (nat_col // B_CHUNK)
        causal = (same_chunk & (nat_col <= nat_row)).astype(F32)
        i = lax.broadcasted_iota(jnp.int32, (SUPER, B_CHUNK), 0)
        p = lax.broadcasted_iota(jnp.int32, (SUPER, B_CHUNK), 1)
        pos = (RUN * (i & (RUN - 1)) + (i >> 4)) % B_CHUNK
        pick = (p == pos).astype(BF16)
        for g in range(groups):
            rows = jnp.dot(pick, ws_ref[0, g].astype(BF16), preferred_element_type=F32).astype(BF16)
            full = lax.dot_general(rows, pick, _NT, preferred_element_type=F32)
            wsp_ref[g] = (full * causal).astype(BF16)

    for rs in _row_blocks(tm):
        vn_ref[rs, :] = _layer_norm(v_ref[rs, :].astype(F32), vg_ref[0], vb_ref[0]).astype(BF16)
        for g in range(groups):
            sl = slice(g * HEAD_DIM, (g + 1) * HEAD_DIM)
            for s0 in range(rs.start, rs.stop, SUPER):
                sb = slice(s0, s0 + SUPER)
                sg = jnp.dot(wsp_ref[g], vn_ref[sb, sl], preferred_element_type=F32) + bs_ref[0, :, g:g + 1]
                cat_ref[sb, sl] = (u_ref[sb, sl].astype(F32) * sg).astype(BF16)
        _mem_attn_outproj_ln(rs, x_ref, wmq_ref, kv_ref, wout_ref, g_ref, b_ref, out_ref, outb_ref, cat_ref,
                             groups * HEAD_DIM)


def _mix_tail_in_specs(d, seq, w_in_shape, layer, kv_shape, wout_shape):
    tiles_per_batch = seq // MIX_TM
    _, _, m, kvw = kv_shape
    k_out, _ = wout_shape
    mq_block = (w_in_shape[1] - MEM_WIDTH) // MEM_WIDTH
    return [
        pl.BlockSpec((MIX_TM, d), lambda i: (i, 0)),
        pl.BlockSpec((d, MEM_WIDTH), lambda i: (0, mq_block)),
        pl.BlockSpec((1, 1, m, kvw), lambda i: (layer, i // tiles_per_batch, 0, 0)),
        pl.BlockSpec((k_out, d), lambda i: (0, 0)),
        pl.BlockSpec((1, 1, d), lambda i: (layer, 0, 0)),
        pl.BlockSpec((1, 1, d), lambda i: (layer, 0, 0)),
    ]


def _mix_a(x, outs, lses, seq, w_in, kv, w_out, ln_g, ln_b, layer):
    t, d = x.shape
    width = outs[0].shape[1]
    tail = _mix_tail_in_specs(d, seq, w_in.shape, layer, kv.shape, w_out.shape)
    return pl.pallas_call(
        functools.partial(_mix_a_kernel, heads=width // HEAD_DIM),
        grid=(t // MIX_TM,),
        in_specs=[pl.BlockSpec((MIX_TM, width), lambda i: (i, 0))] * 3
                 + [pl.BlockSpec((MIX_TM, LANES), lambda i: (i, 0))] * 3 + tail,
        out_specs=[pl.BlockSpec((MIX_TM, d), lambda i: (i, 0))] * 2,
        out_shape=[jax.ShapeDtypeStruct((t, d), F32), jax.ShapeDtypeStruct((t, d), BF16)],
        scratch_shapes=[pltpu.VMEM((MIX_TM, w_out.shape[0]), BF16)],
        compiler_params=_params("parallel"),
        name="mix_a",
    )(*outs, *lses, x, w_in, kv, w_out, ln_g, ln_b)


def _mix_b(x, z, seq, w_s, bs_perm, vn_g, vn_b, layer_in, w_in, kv, w_out, ln_g, ln_b, layer):
    t, d = x.shape
    width = z.shape[1] // 2
    groups = width // HEAD_DIM
    tail = _mix_tail_in_specs(d, seq, w_in.shape, layer, kv.shape, w_out.shape)
    return pl.pallas_call(
        functools.partial(_mix_b_kernel, groups=groups),
        grid=(t // MIX_TM,),
        in_specs=[pl.BlockSpec((MIX_TM, width), lambda i: (i, 0)),
                  pl.BlockSpec((MIX_TM, width), lambda i: (i, 1)),
                  pl.BlockSpec((1, groups, B_CHUNK, B_CHUNK), lambda i: (layer_in, 0, 0, 0)),
                  pl.BlockSpec((1, SUPER, groups), lambda i: (layer_in, 0, 0)),
                  pl.BlockSpec((1, 1, width), lambda i: (layer_in, 0, 0)),
                  pl.BlockSpec((1, 1, width), lambda i: (layer_in, 0, 0))] + tail,
        out_specs=[pl.BlockSpec((MIX_TM, d), lambda i: (i, 0))] * 2,
        out_shape=[jax.ShapeDtypeStruct((t, d), F32), jax.ShapeDtypeStruct((t, d), BF16)],
        scratch_shapes=[pltpu.VMEM((MIX_TM, w_out.shape[0]), BF16),
                        pltpu.VMEM((MIX_TM, width), BF16),
                        pltpu.VMEM((groups, SUPER, SUPER), BF16)],
        compiler_params=_params("arbitrary"),
        name="mix_b",
    )(z, z, w_s, bs_perm, vn_g, vn_b, x, w_in, kv, w_out, ln_g, ln_b)


def _ffn_kernel(x_ref, xb_ref, w1_ref, w2_ref, g_ref, b_ref, o_ref, acc0_ref, acc1_ref, *, side_work):
    i, j = pl.program_id(0), pl.program_id(1)
    n_tiles = pl.num_programs(0) - 1
    rows = x_ref.shape[0]

    @pl.when((i == 0) & (j == 0))
    def _():
        acc0_ref[...] = jnp.zeros_like(acc0_ref)
        acc1_ref[...] = jnp.zeros_like(acc1_ref)

    def finish_rows(acc_prev):
        sl = pl.ds(pl.multiple_of(j * rows, rows), rows)
        done = _layer_norm(ALPHA * x_ref[...] + acc_prev[sl, :], g_ref[0], b_ref[0])
        o_ref[...] = done
        acc_prev[sl, :] = jnp.zeros((rows, acc_prev.shape[1]), F32)
        return done

    def tile_step(acc_cur, acc_prev):
        side_work()
        done = finish_rows(acc_prev)
        word = pltpu.bitcast(jnp.max(done, axis=0, keepdims=True)[:, 0:LANES], jnp.uint32)
        zero = ((word >> 16) >> 16).astype(F32)
        h = jnp.maximum(jnp.dot(xb_ref[...], w1_ref[...], preferred_element_type=F32), 0.0)
        h = h * h + zero[:, 0:1]
        acc_cur[...] += jnp.dot(h.astype(BF16), w2_ref[...], preferred_element_type=F32)

    @pl.when((i < n_tiles) & (lax.rem(i, 2) == 0))
    def _():
        tile_step(acc0_ref, acc1_ref)

    @pl.when((i < n_tiles) & (lax.rem(i, 2) == 1))
    def _():
        tile_step(acc1_ref, acc0_ref)

    @pl.when((i == n_tiles) & (lax.rem(i, 2) == 0))
    def _():
        finish_rows(acc1_ref)

    @pl.when((i == n_tiles) & (lax.rem(i, 2) == 1))
    def _():
        finish_rows(acc0_ref)


def _ffn(x, xb, w1, w2, ln_g, ln_b, layer, cast_jobs=()):
    t, d = x.shape
    d_ff = w1.shape[1]
    n_i, n_j = t // FFN_TM, d_ff // FFN_TF
    rows = FFN_TM // n_j
    assert rows % 8 == 0

    def finished(i, j):
        return (jnp.maximum(i - 1, 0) * n_j + j, 0)

    def w_tile(i, j):
        return jnp.where(i == n_i, n_j - 1, j)

    c_in, c_out, c_shape = _cast_specs(cast_jobs, lambda i, j: i * n_j + j)
    out, *casts = pl.pallas_call(
        _with_casts(_ffn_kernel, 6, 1, len(cast_jobs)),
        grid=(n_i + 1, n_j),
        in_specs=[pl.BlockSpec((rows, d), finished),
                  pl.BlockSpec((FFN_TM, d), lambda i, j: (jnp.minimum(i, n_i - 1), 0)),
                  pl.BlockSpec((d, FFN_TF), lambda i, j: (0, w_tile(i, j))),
                  pl.BlockSpec((FFN_TF, d), lambda i, j: (w_tile(i, j), 0)),
                  pl.BlockSpec((1, 1, d), lambda i, j: (layer, 0, 0)),
                  pl.BlockSpec((1, 1, d), lambda i, j: (layer, 0, 0))] + c_in,
        out_specs=[pl.BlockSpec((rows, d), finished)] + c_out,
        out_shape=[jax.ShapeDtypeStruct((t, d), F32)] + c_shape,
        scratch_shapes=[pltpu.VMEM((FFN_TM, d), F32), pltpu.VMEM((FFN_TM, d), F32)],
        compiler_params=_params("arbitrary", "arbitrary"),
        name="ffn",
    )(x, xb, w1, w2, ln_g, ln_b, *[job[0] for job in cast_jobs])
    return out, casts


def _gmlp_bias_in_working_order(b_s):
    n_layers, groups, chunk = b_s.shape
    half = chunk // RUN
    reps = SUPER // chunk
    b = b_s.reshape(n_layers, groups, half, RUN).transpose(0, 3, 2, 1)
    b = jnp.broadcast_to(b[:, :, None], (n_layers, RUN, reps, half, groups))
    return b.reshape(n_layers, SUPER, groups)


def kernel(x, mem, w_in_a, w_in_b, w_s, b_s, vnorm_g, vnorm_b, w_mem_kv, w_out, ln1_g, ln1_b, w_ff1, w_ff2,
           ln2_g, ln2_b):
    batch, seq, d = x.shape
    row3 = lambda a: a.reshape(a.shape[0], 1, a.shape[1])
    ln1_g, ln1_b, ln2_g, ln2_b = row3(ln1_g), row3(ln1_b), row3(ln2_g), row3(ln2_b)
    vnorm_g, vnorm_b = row3(vnorm_g), row3(vnorm_b)
    bs_perm = _gmlp_bias_in_working_order(b_s)
    a_cols = w_in_a.shape[2] - MEM_WIDTH
    b_cols = w_in_b.shape[2] - MEM_WIDTH

    def w_in_of(layer):
        return (w_in_a, layer // 2) if layer % 2 == 0 else (w_in_b, layer // 2)

    xf = _perm(x.reshape(batch * seq, d))
    kv = _memkv(mem, w_mem_kv)
    w_in = w_in_a[0].astype(BF16)
    w_o = w_1 = w_2 = None
    for i in range(DEPTH):
        if i % 2 == 0:
            jobs = [(w_out, 0, 48), (w_ff1, 0, 64), (w_ff2, 0, 64)] if i == 0 else []
            qkv, casts = _proj(xf, w_in, a_cols, gelu=False, cast_jobs=jobs)
            if i == 0:
                w_o, w_1, w_2 = casts
            outs, lses = [], []
            for g, (window, dilation) in enumerate(A_PAIRS):
                o, lse = _attn_group(qkv, batch, seq, g, window, dilation)
                outs.append(o)
                lses.append(lse)
            xf, xb = _mix_a(xf, outs, lses, seq, w_in, kv, w_o, ln1_g, ln1_b, i)
        else:
            z, _ = _proj(xf, w_in, b_cols, gelu=True)
            xf, xb = _mix_b(xf, z, seq, w_s, bs_perm, vnorm_g, vnorm_b, i // 2, w_in, kv, w_o, ln1_g, ln1_b, i)
        if i + 1 < DEPTH:
            nxt_in, nxt_layer = w_in_of(i + 1)
            jobs = [(nxt_in, nxt_layer, 128), (w_out, i + 1, 96), (w_ff1, i + 1, 128), (w_ff2, i + 1, 128)]
            xf, (w_in, w_o, w_1, w_2) = _ffn(xf, xb, w_1, w_2, ln2_g, ln2_b, i, cast_jobs=jobs)
        else:
            xf, _ = _ffn(xf, xb, w_1, w_2, ln2_g, ln2_b, i)
    return _perm(xf).reshape(batch, seq, d)
```

```python
import functools

import numpy as np
import jax
import jax.numpy as jnp
from jax import lax
from jax.experimental import pallas as pl
from jax.experimental.pallas import tpu as pltpu

F32 = jnp.float32
BF16 = jnp.bfloat16

HEAD_DIM = 128
A_PAIRS = ((128, 1), (512, 4), (2048, 16))
A_GROUPS = len(A_PAIRS)
A_WINDOW = 128
B_CHUNK = 128
MEM_HEADS = 4
MEM_WIDTH = MEM_HEADS * HEAD_DIM
DEPTH = 4
ALPHA = (2.0 * DEPTH) ** 0.25
LN_EPS = 1e-5
SCALE = HEAD_DIM ** -0.5
SQRT_HALF = np.float32(np.sqrt(0.5))

LANES = 128
VMEM_LIMIT_BYTES = 56 * 1024 * 1024

RUN = 16
SUPER = RUN * RUN

PERM_ROWS = 4 * SUPER
PROJ_TM, PROJ_TN = 1024, 1024
ATTN_ROWS = 1024
MIX_TM = 512
MIX_SUB = MIX_TM
FFN_TM, FFN_TF = 1024, 1024

_NT = (((1,), (1,)), ((), ()))


def _params(*sem):
    return pltpu.CompilerParams(dimension_semantics=sem, vmem_limit_bytes=VMEM_LIMIT_BYTES)


def _cast_specs(jobs, step_of):
    in_specs, out_specs, out_shapes = [], [], []
    for src, layer, n_blocks in jobs:
        _, rows, cols = src.shape
        assert rows % n_blocks == 0 and (rows // n_blocks) % RUN == 0
        blk = rows // n_blocks
        in_specs.append(pl.BlockSpec(
            (1, blk, cols), lambda *ids, _l=layer, _n=n_blocks: (_l, jnp.minimum(step_of(*ids), _n - 1), 0)))
        out_specs.append(pl.BlockSpec(
            (blk, cols), lambda *ids, _n=n_blocks: (jnp.minimum(step_of(*ids), _n - 1), 0)))
        out_shapes.append(jax.ShapeDtypeStruct((rows, cols), BF16))
    return in_specs, out_specs, out_shapes


def _with_casts(body, n_in, n_out, n_cast):
    def kernel(*refs):
        ins, refs = refs[:n_in], refs[n_in:]
        cast_src, refs = refs[:n_cast], refs[n_cast:]
        outs, refs = refs[:n_out], refs[n_out:]
        cast_dst, scratch = refs[:n_cast], refs[n_cast:]

        def side_work():
            for src, dst in zip(cast_src, cast_dst):
                dst[...] = src[0].astype(BF16)

        body(*ins, *outs, *scratch, side_work=side_work)
    return kernel


def _layer_norm(x, g, b):
    mu = jnp.mean(x, axis=-1, keepdims=True)
    xc = x - mu
    var = jnp.mean(xc * xc, axis=-1, keepdims=True)
    return xc * lax.rsqrt(var + LN_EPS) * g + b


def _perm_kernel(x_ref, o_ref, slab_ref):
    n_slabs, rows, _ = slab_ref.shape
    for c in range(n_slabs):
        slab_ref[c] = x_ref[:, c * LANES:(c + 1) * LANES]

    def one_superblock(sb, carry):
        base = pl.multiple_of(sb * SUPER, SUPER)
        for r in range(RUN):
            for c in range(n_slabs):
                o_ref[pl.ds(base + r * RUN, RUN), c * LANES:(c + 1) * LANES] = (
                    slab_ref[c, pl.ds(base + r, RUN, stride=RUN), :])
        return carry

    lax.fori_loop(0, rows // SUPER, one_superblock, 0)


def _perm(x):
    t, d = x.shape
    return pl.pallas_call(
        _perm_kernel,
        grid=(t // PERM_ROWS,),
        in_specs=[pl.BlockSpec((PERM_ROWS, d), lambda i: (i, 0))],
        out_specs=pl.BlockSpec((PERM_ROWS, d), lambda i: (i, 0)),
        out_shape=jax.ShapeDtypeStruct((t, d), x.dtype),
        scratch_shapes=[pltpu.VMEM((d // LANES, PERM_ROWS, LANES), x.dtype)],
        compiler_params=_params("parallel"),
        name="perm",
    )(x)


def _memkv_kernel(mem_ref, w_ref, o_ref):
    o_ref[0, 0] = jnp.dot(mem_ref[0].astype(BF16), w_ref[0].astype(BF16),
                          preferred_element_type=F32).astype(BF16)


def _memkv(mem, w_mem_kv):
    nb, m, d = mem.shape
    depth, _, n = w_mem_kv.shape
    return pl.pallas_call(
        _memkv_kernel,
        grid=(depth, nb),
        in_specs=[pl.BlockSpec((1, m, d), lambda i, b: (b, 0, 0)),
                  pl.BlockSpec((1, d, n), lambda i, b: (i, 0, 0))],
        out_specs=pl.BlockSpec((1, 1, m, n), lambda i, b: (i, b, 0, 0)),
        out_shape=jax.ShapeDtypeStruct((depth, nb, m, n), BF16),
        compiler_params=_params("parallel", "arbitrary"),
        name="memkv",
    )(mem, w_mem_kv)


def _proj_kernel(x_ref, w_ref, o_ref, xb_ref, *, gelu, side_work):
    @pl.when(pl.program_id(1) == 0)
    def _():
        xb_ref[...] = x_ref[...].astype(BF16)

    side_work()
    acc = jnp.dot(xb_ref[...], w_ref[...], preferred_element_type=F32)
    if gelu:
        acc = 0.5 * acc * (1.0 + lax.erf(acc * SQRT_HALF))
    o_ref[...] = acc.astype(o_ref.dtype)


def _proj(x, w, n_out, gelu, cast_jobs=()):
    t, d = x.shape
    n_j = n_out // PROJ_TN
    c_in, c_out, c_shape = _cast_specs(cast_jobs, lambda i, j: i * n_j + j)
    out, *casts = pl.pallas_call(
        _with_casts(functools.partial(_proj_kernel, gelu=gelu), 2, 1, len(cast_jobs)),
        grid=(t // PROJ_TM, n_j),
        in_specs=[pl.BlockSpec((PROJ_TM, d), lambda i, j: (i, 0)),
                  pl.BlockSpec((d, PROJ_TN), lambda i, j: (0, j))] + c_in,
        out_specs=[pl.BlockSpec((PROJ_TM, PROJ_TN), lambda i, j: (i, j))] + c_out,
        out_shape=[jax.ShapeDtypeStruct((t, n_out), BF16)] + c_shape,
        scratch_shapes=[pltpu.VMEM((PROJ_TM, d), BF16)],
        compiler_params=_params("arbitrary", "arbitrary"),
        name="proj_gelu" if gelu else "proj",
    )(x, w, *[job[0] for job in cast_jobs])
    return out, casts


def _class_pos(i, dilation):
    per_super = SUPER // dilation
    n_a = RUN // dilation
    return (i & ~(per_super - 1)) + n_a * (i & (RUN - 1)) + ((i >> 4) & (n_a - 1))


def _attn_kernel(q_ref, kc_ref, kp_ref, vc_ref, vp_ref, o_ref, lse_ref,
                 qbuf, kbuf, vbuf, obuf, lbuf, bias_ref, *, rows, unit, tile, heads, dilation):
    n_keys = 2 * unit
    width = heads * HEAD_DIM
    chunk = pl.program_id(2)
    first_step = (pl.program_id(0) == 0) & (pl.program_id(1) == 0) & (chunk == 0)

    @pl.when(first_step)
    def _():
        neg_inf = jnp.float32(-jnp.inf)
        for qt in range(unit // tile):
            row = lax.broadcasted_iota(jnp.int32, (tile, n_keys), 0) + qt * tile
            col = lax.broadcasted_iota(jnp.int32, (tile, n_keys), 1)
            dist = unit + _class_pos(row, dilation) - (_class_pos(col & (unit - 1), dilation) + (col & unit))
            band = (dist >= 0) & (dist <= A_WINDOW)
            bias_ref[0, qt] = jnp.where(band & (col >= unit), 0.0, neg_inf)
            bias_ref[1, qt] = jnp.where(band, 0.0, neg_inf)

    qbuf[...] = q_ref[...].reshape(rows, width)
    kbuf[0:unit, :] = kp_ref[...].reshape(unit, width)
    kbuf[unit:, :] = kc_ref[...].reshape(rows, width)
    vbuf[0:unit, :] = vp_ref[...].reshape(unit, width)
    vbuf[unit:, :] = vc_ref[...].reshape(rows, width)

    has_prev = jnp.where(chunk > 0, 1, 0)
    lane = lax.broadcasted_iota(jnp.int32, (tile, LANES), 1)
    for u in range(rows // unit):
        k0 = u * unit
        for qt in range(unit // tile):
            bias = bias_ref[has_prev if u == 0 else 1, qt]
            r0 = k0 + qt * tile
            lse_tile = jnp.zeros((tile, LANES), F32)
            for h in range(heads):
                cs = slice(h * HEAD_DIM, (h + 1) * HEAD_DIM)
                s = lax.dot_general(qbuf[r0:r0 + tile, cs], kbuf[k0:k0 + n_keys, cs], _NT,
                                    preferred_element_type=F32) * SCALE + bias
                m = jnp.max(s, axis=-1, keepdims=True)
                p = jnp.exp(s - m)
                l = jnp.sum(p, axis=-1, keepdims=True)
                o = jnp.dot(p.astype(BF16), vbuf[k0:k0 + n_keys, cs], preferred_element_type=F32) / l
                obuf[r0:r0 + tile, cs] = o.astype(BF16)
                lse_tile = jnp.where(lane == h, m + jnp.log(l), lse_tile)
            lbuf[r0:r0 + tile, :] = lse_tile
    o_ref[...] = obuf[...].reshape(o_ref.shape)
    lse_ref[...] = lbuf[...].reshape(lse_ref.shape)


def _attn_group(qkv, batch, seq, group, window, dilation):
    t, cols = qkv.shape
    width = cols // (A_GROUPS * 3)
    heads = width // HEAD_DIM
    assert window // dilation == A_WINDOW and RUN % dilation == 0
    n_super = seq // SUPER
    n_a = RUN // dilation
    per_super = SUPER // dilation
    unit = max(A_WINDOW, per_super)
    rows = min(ATTN_ROWS, seq // dilation)
    assert rows % unit == 0 and unit % per_super == 0
    tile = unit

    def view(a, last):
        return a.reshape(batch, n_super, n_a, dilation, RUN, last)

    def cur(last, col_block):
        return pl.BlockSpec((None, rows // per_super, n_a, None, RUN, last),
                            lambda b, r, c: (b, c, 0, r, 0, col_block))

    def prev(col_block):
        return pl.BlockSpec((None, unit // per_super, n_a, None, RUN, width),
                            lambda b, r, c: (b, jnp.maximum(c * (rows // unit) - 1, 0), 0, r, 0, col_block))

    qkv_v = view(qkv, cols)
    q_col, k_col, v_col = (group * 3 + i for i in range(3))
    o, lse = pl.pallas_call(
        functools.partial(_attn_kernel, rows=rows, unit=unit, tile=tile, heads=heads, dilation=dilation),
        grid=(batch, dilation, (seq // dilation) // rows),
        in_specs=[cur(width, q_col), cur(width, k_col), prev(k_col), cur(width, v_col), prev(v_col)],
        out_specs=[cur(width, 0), cur(LANES, 0)],
        out_shape=[jax.ShapeDtypeStruct((batch, n_super, n_a, dilation, RUN, width), BF16),
                   jax.ShapeDtypeStruct((batch, n_super, n_a, dilation, RUN, LANES), F32)],
        scratch_shapes=[pltpu.VMEM((rows, width), BF16),
                        pltpu.VMEM((rows + unit, width), BF16),
                        pltpu.VMEM((rows + unit, width), BF16),
                        pltpu.VMEM((rows, width), BF16),
                        pltpu.VMEM((rows, LANES), F32),
                        pltpu.VMEM((2, unit // tile, tile, 2 * unit), F32)],
        compiler_params=_params("arbitrary", "arbitrary", "arbitrary"),
        name=f"attn_d{dilation}",
    )(qkv_v, qkv_v, qkv_v, qkv_v, qkv_v)
    return o.reshape(t, width), lse.reshape(t, LANES)


def _mem_attn_outproj_ln(rs, x_ref, wmq_ref, kv_ref, wout_ref, g_ref, b_ref, out_ref, outb_ref, cat_ref,
                         mix_width):
    xf = x_ref[rs, :]
    y = jnp.dot(cat_ref[rs, 0:mix_width], wout_ref[0:mix_width, :], preferred_element_type=F32)
    mq = jnp.dot(xf.astype(BF16), wmq_ref[...], preferred_element_type=F32).astype(BF16)
    for h in range(MEM_HEADS):
        c0 = h * HEAD_DIM
        k = kv_ref[0, 0, :, c0:c0 + HEAD_DIM]
        v = kv_ref[0, 0, :, MEM_WIDTH + c0:MEM_WIDTH + c0 + HEAD_DIM]
        s = lax.dot_general(mq[:, c0:c0 + HEAD_DIM], k, _NT, preferred_element_type=F32) * SCALE
        p = jnp.exp(s - jnp.max(s, axis=-1, keepdims=True))
        l = jnp.sum(p, axis=-1, keepdims=True)
        o = jnp.dot(p.astype(BF16), v, preferred_element_type=F32) / l
        cat_ref[rs, mix_width + c0:mix_width + c0 + HEAD_DIM] = o.astype(BF16)
    y = y + jnp.dot(cat_ref[rs, mix_width:], wout_ref[mix_width:, :], preferred_element_type=F32)
    x1 = _layer_norm(ALPHA * xf + y, g_ref[0], b_ref[0])
    out_ref[rs, :] = x1
    outb_ref[rs, :] = x1.astype(BF16)


def _row_blocks(tm):
    return [slice(r, r + MIX_SUB) for r in range(0, tm, MIX_SUB)]


def _mix_a_kernel(o0_ref, o1_ref, o2_ref, l0_ref, l1_ref, l2_ref, x_ref, wmq_ref, kv_ref, wout_ref,
                  g_ref, b_ref, out_ref, outb_ref, cat_ref, *, heads):
    for rs in _row_blocks(x_ref.shape[0]):
        l0, l1, l2 = l0_ref[rs, :], l1_ref[rs, :], l2_ref[rs, :]
        mx = jnp.maximum(jnp.maximum(l0, l1), l2)
        e0, e1, e2 = jnp.exp(l0 - mx), jnp.exp(l1 - mx), jnp.exp(l2 - mx)
        den = e0 + e1 + e2
        w0, w1 = e0 / den, e1 / den
        for h in range(heads):
            sl = slice(h * HEAD_DIM, (h + 1) * HEAD_DIM)
            o2 = o2_ref[rs, sl].astype(F32)
            mix = (o2 + w0[:, h:h + 1] * (o0_ref[rs, sl].astype(F32) - o2)
                   + w1[:, h:h + 1] * (o1_ref[rs, sl].astype(F32) - o2))
            cat_ref[rs, sl] = mix.astype(BF16)
        _mem_attn_outproj_ln(rs, x_ref, wmq_ref, kv_ref, wout_ref, g_ref, b_ref, out_ref, outb_ref, cat_ref,
                             heads * HEAD_DIM)


def _mix_b_kernel(u_ref, v_ref, ws_ref, bs_ref, vg_ref, vb_ref, x_ref, wmq_ref, kv_ref, wout_ref,
                  g_ref, b_ref, out_ref, outb_ref, cat_ref, vn_ref, wsp_ref, *, groups):
    tm = u_ref.shape[0]

    @pl.when(pl.program_id(0) == 0)
    def _():
        row = lax.broadcasted_iota(jnp.int32, (SUPER, SUPER), 0)
        col = lax.broadcasted_iota(jnp.int32, (SUPER, SUPER), 1)
        nat_row = RUN * (row & (RUN - 1)) + (row >> 4)
        nat_col = RUN * (col & (RUN - 1)) + (col >> 4)
        same_chunk = (nat_row // B_CHUNK) == (nat_col // B_CHUNK)
        causal = (same_chunk & (nat_col <= nat_row)).astype(F32)
        i = lax.broadcasted_iota(jnp.int32, (SUPER, B_CHUNK), 0)
        p = lax.broadcasted_iota(jnp.int32, (SUPER, B_CHUNK), 1)
        pos = (RUN * (i & (RUN - 1)) + (i >> 4)) % B_CHUNK
        pick = (p == pos).astype(BF16)
        for g in range(groups):
            rows = jnp.dot(pick, ws_ref[0, g].astype(BF16), preferred_element_type=F32).astype(BF16)
            full = lax.dot_general(rows, pick, _NT, preferred_element_type=F32)
            wsp_ref[g] = (full * causal).astype(BF16)

    for rs in _row_blocks(tm):
        vn_ref[rs, :] = _layer_norm(v_ref[rs, :].astype(F32), vg_ref[0], vb_ref[0]).astype(BF16)
        for g in range(groups):
            sl = slice(g * HEAD_DIM, (g + 1) * HEAD_DIM)
            for s0 in range(rs.start, rs.stop, SUPER):
                sb = slice(s0, s0 + SUPER)
                sg = jnp.dot(wsp_ref[g], vn_ref[sb, sl], preferred_element_type=F32) + bs_ref[0, :, g:g + 1]
                cat_ref[sb, sl] = (u_ref[sb, sl].astype(F32) * sg).astype(BF16)
        _mem_attn_outproj_ln(rs, x_ref, wmq_ref, kv_ref, wout_ref, g_ref, b_ref, out_ref, outb_ref, cat_ref,
                             groups * HEAD_DIM)


def _mix_tail_in_specs(d, seq, w_in_shape, layer, kv_shape, wout_shape):
    tiles_per_batch = seq // MIX_TM
    _, _, m, kvw = kv_shape
    k_out, _ = wout_shape
    mq_block = (w_in_shape[1] - MEM_WIDTH) // MEM_WIDTH
    return [
        pl.BlockSpec((MIX_TM, d), lambda i: (i, 0)),
        pl.BlockSpec((d, MEM_WIDTH), lambda i: (0, mq_block)),
        pl.BlockSpec((1, 1, m, kvw), lambda i: (layer, i // tiles_per_batch, 0, 0)),
        pl.BlockSpec((k_out, d), lambda i: (0, 0)),
        pl.BlockSpec((1, 1, d), lambda i: (layer, 0, 0)),
        pl.BlockSpec((1, 1, d), lambda i: (layer, 0, 0)),
    ]


def _mix_a(x, outs, lses, seq, w_in, kv, w_out, ln_g, ln_b, layer):
    t, d = x.shape
    width = outs[0].shape[1]
    tail = _mix_tail_in_specs(d, seq, w_in.shape, layer, kv.shape, w_out.shape)
    return pl.pallas_call(
        functools.partial(_mix_a_kernel, heads=width // HEAD_DIM),
        grid=(t // MIX_TM,),
        in_specs=[pl.BlockSpec((MIX_TM, width), lambda i: (i, 0))] * 3
                 + [pl.BlockSpec((MIX_TM, LANES), lambda i: (i, 0))] * 3 + tail,
        out_specs=[pl.BlockSpec((MIX_TM, d), lambda i: (i, 0))] * 2,
        out_shape=[jax.ShapeDtypeStruct((t, d), F32), jax.ShapeDtypeStruct((t, d), BF16)],
        scratch_shapes=[pltpu.VMEM((MIX_TM, w_out.shape[0]), BF16)],
        compiler_params=_params("parallel"),
        name="mix_a",
    )(*outs, *lses, x, w_in, kv, w_out, ln_g, ln_b)


def _mix_b(x, z, seq, w_s, bs_perm, vn_g, vn_b, layer_in, w_in, kv, w_out, ln_g, ln_b, layer):
    t, d = x.shape
    width = z.shape[1] // 2
    groups = width // HEAD_DIM
    tail = _mix_tail_in_specs(d, seq, w_in.shape, layer, kv.shape, w_out.shape)
    return pl.pallas_call(
        functools.partial(_mix_b_kernel, groups=groups),
        grid=(t // MIX_TM,),
        in_specs=[pl.BlockSpec((MIX_TM, width), lambda i: (i, 0)),
                  pl.BlockSpec((MIX_TM, width), lambda i: (i, 1)),
                  pl.BlockSpec((1, groups, B_CHUNK, B_CHUNK), lambda i: (layer_in, 0, 0, 0)),
                  pl.BlockSpec((1, SUPER, groups), lambda i: (layer_in, 0, 0)),
                  pl.BlockSpec((1, 1, width), lambda i: (layer_in, 0, 0)),
                  pl.BlockSpec((1, 1, width), lambda i: (layer_in, 0, 0))] + tail,
        out_specs=[pl.BlockSpec((MIX_TM, d), lambda i: (i, 0))] * 2,
        out_shape=[jax.ShapeDtypeStruct((t, d), F32), jax.ShapeDtypeStruct((t, d), BF16)],
        scratch_shapes=[pltpu.VMEM((MIX_TM, w_out.shape[0]), BF16),
                        pltpu.VMEM((MIX_TM, width), BF16),
                        pltpu.VMEM((groups, SUPER, SUPER), BF16)],
        compiler_params=_params("arbitrary"),
        name="mix_b",
    )(z, z, w_s, bs_perm, vn_g, vn_b, x, w_in, kv, w_out, ln_g, ln_b)


def _ffn_kernel(x_ref, xb_ref, w1_ref, w2_ref, g_ref, b_ref, o_ref, acc0_ref, acc1_ref, *, side_work):
    i, j = pl.program_id(0), pl.program_id(1)
    n_tiles = pl.num_programs(0) - 1
    rows = x_ref.shape[0]

    @pl.when((i == 0) & (j == 0))
    def _():
        acc0_ref[...] = jnp.zeros_like(acc0_ref)
        acc1_ref[...] = jnp.zeros_like(acc1_ref)

    def finish_rows(acc_prev):
        sl = pl.ds(pl.multiple_of(j * rows, rows), rows)
        done = _layer_norm(ALPHA * x_ref[...] + acc_prev[sl, :], g_ref[0], b_ref[0])
        o_ref[...] = done
        acc_prev[sl, :] = jnp.zeros((rows, acc_prev.shape[1]), F32)
        return done

    def tile_step(acc_cur, acc_prev):
        side_work()
        done = finish_rows(acc_prev)
        word = pltpu.bitcast(jnp.max(done, axis=0, keepdims=True)[:, 0:LANES], jnp.uint32)
        zero = ((word >> 16) >> 16).astype(F32)
        h = jnp.maximum(jnp.dot(xb_ref[...], w1_ref[...], preferred_element_type=F32), 0.0)
        h = h * h + zero[:, 0:1]
        acc_cur[...] += jnp.dot(h.astype(BF16), w2_ref[...], preferred_element_type=F32)

    @pl.when((i < n_tiles) & (lax.rem(i, 2) == 0))
    def _():
        tile_step(acc0_ref, acc1_ref)

    @pl.when((i < n_tiles) & (lax.rem(i, 2) == 1))
    def _():
        tile_step(acc1_ref, acc0_ref)

    @pl.when((i == n_tiles) & (lax.rem(i, 2) == 0))
    def _():
        finish_rows(acc1_ref)

    @pl.when((i == n_tiles) & (lax.rem(i, 2) == 1))
    def _():
        finish_rows(acc0_ref)


def _ffn(x, xb, w1, w2, ln_g, ln_b, layer, cast_jobs=()):
    t, d = x.shape
    d_ff = w1.shape[1]
    n_i, n_j = t // FFN_TM, d_ff // FFN_TF
    rows = FFN_TM // n_j
    assert rows % 8 == 0

    def finished(i, j):
        return (jnp.where(i == 0, 0, (i - 1) * n_j + j), 0)

    def w_tile(i, j):
        return jnp.where(i == n_i, n_j - 1, j)

    c_in, c_out, c_shape = _cast_specs(cast_jobs, lambda i, j: i * n_j + j)
    out, *casts = pl.pallas_call(
        _with_casts(_ffn_kernel, 6, 1, len(cast_jobs)),
        grid=(n_i + 1, n_j),
        in_specs=[pl.BlockSpec((rows, d), finished),
                  pl.BlockSpec((FFN_TM, d), lambda i, j: (jnp.minimum(i, n_i - 1), 0)),
                  pl.BlockSpec((d, FFN_TF), lambda i, j: (0, w_tile(i, j))),
                  pl.BlockSpec((FFN_TF, d), lambda i, j: (w_tile(i, j), 0)),
                  pl.BlockSpec((1, 1, d), lambda i, j: (layer, 0, 0)),
                  pl.BlockSpec((1, 1, d), lambda i, j: (layer, 0, 0))] + c_in,
        out_specs=[pl.BlockSpec((rows, d), finished)] + c_out,
        out_shape=[jax.ShapeDtypeStruct((t, d), F32)] + c_shape,
        scratch_shapes=[pltpu.VMEM((FFN_TM, d), F32), pltpu.VMEM((FFN_TM, d), F32)],
        compiler_params=_params("arbitrary", "arbitrary"),
        name="ffn",
    )(x, xb, w1, w2, ln_g, ln_b, *[job[0] for job in cast_jobs])
    return out, casts


def _gmlp_bias_in_working_order(b_s):
    n_layers, groups, chunk = b_s.shape
    half = chunk // RUN
    reps = SUPER // chunk
    b = b_s.reshape(n_layers, groups, half, RUN).transpose(0, 3, 2, 1)
    b = jnp.broadcast_to(b[:, :, None], (n_layers, RUN, reps, half, groups))
    return b.reshape(n_layers, SUPER, groups)


def kernel(x, mem, w_in_a, w_in_b, w_s, b_s, vnorm_g, vnorm_b, w_mem_kv, w_out, ln1_g, ln1_b, w_ff1, w_ff2,
           ln2_g, ln2_b):
    batch, seq, d = x.shape
    row3 = lambda a: a.reshape(a.shape[0], 1, a.shape[1])
    ln1_g, ln1_b, ln2_g, ln2_b = row3(ln1_g), row3(ln1_b), row3(ln2_g), row3(ln2_b)
    vnorm_g, vnorm_b = row3(vnorm_g), row3(vnorm_b)
    bs_perm = _gmlp_bias_in_working_order(b_s)
    a_cols = w_in_a.shape[2] - MEM_WIDTH
    b_cols = w_in_b.shape[2] - MEM_WIDTH

    def cast_job(kind, layer):
        if kind == "in":
            return (w_in_a if layer % 2 == 0 else w_in_b, layer // 2, 64)
        return {"out": (w_out, layer, 48), "ff1": (w_ff1, layer, 64), "ff2": (w_ff2, layer, 64)}[kind]

    hosted = {("proj", 0): [("out", 0), ("ff1", 0), ("ff2", 0), ("in", 1), ("out", 1), ("in", 2)],
              ("ffn", 0): [("ff1", 1), ("ff2", 1)],
              ("ffn", 1): [("ff1", 2), ("ff2", 2)],
              ("proj", 2): [("out", 2), ("out", 3), ("in", 3)],
              ("ffn", 2): [("ff1", 3), ("ff2", 3)]}
    bf16 = {("in", 0): w_in_a[0].astype(BF16)}

    def jobs_of(host):
        return [cast_job(*key) for key in hosted.get(host, [])]

    def collect(host, casts):
        bf16.update(zip(hosted.get(host, []), casts))

    xf = _perm(x.reshape(batch * seq, d))
    kv = _memkv(mem, w_mem_kv)
    for i in range(DEPTH):
        w_in = bf16["in", i]
        if i % 2 == 0:
            qkv, casts = _proj(xf, w_in, a_cols, gelu=False, cast_jobs=jobs_of(("proj", i)))
            collect(("proj", i), casts)
            outs, lses = [], []
            for g, (window, dilation) in enumerate(A_PAIRS):
                o, lse = _attn_group(qkv, batch, seq, g, window, dilation)
                outs.append(o)
                lses.append(lse)
            xf, xb = _mix_a(xf, outs, lses, seq, w_in, kv, bf16["out", i], ln1_g, ln1_b, i)
        else:
            z, _ = _proj(xf, w_in, b_cols, gelu=True)
            xf, xb = _mix_b(xf, z, seq, w_s, bs_perm, vnorm_g, vnorm_b, i // 2, w_in, kv, bf16["out", i],
                            ln1_g, ln1_b, i)
        xf, casts = _ffn(xf, xb, bf16["ff1", i], bf16["ff2", i], ln2_g, ln2_b, i, cast_jobs=jobs_of(("ffn", i)))
        collect(("ffn", i), casts)
    return _perm(xf).reshape(batch, seq, d)
```

```python
import functools

import numpy as np
import jax
import jax.numpy as jnp
from jax import lax
from jax.experimental import pallas as pl
from jax.experimental.pallas import tpu as pltpu

F32 = jnp.float32
BF16 = jnp.bfloat16

HEAD_DIM = 128
A_PAIRS = ((128, 1), (512, 4), (2048, 16))
A_GROUPS = len(A_PAIRS)
A_WINDOW = 128
B_CHUNK = 128
MEM_HEADS = 4
MEM_WIDTH = MEM_HEADS * HEAD_DIM
DEPTH = 4
ALPHA = (2.0 * DEPTH) ** 0.25
LN_EPS = 1e-5
SCALE = HEAD_DIM ** -0.5
SQRT_HALF = np.float32(np.sqrt(0.5))

LANES = 128
VMEM_LIMIT_BYTES = 56 * 1024 * 1024

RUN = 16
SUPER = RUN * RUN

PERM_ROWS = 4 * SUPER
PROJ_TM, PROJ_TN = 1024, 1024
ATTN_ROWS = 1024
MIX_TM = 512
MIX_SUB = MIX_TM
FFN_TM, FFN_TF = 1024, 1024

_NT = (((1,), (1,)), ((), ()))


def _params(*sem):
    return pltpu.CompilerParams(dimension_semantics=sem, vmem_limit_bytes=VMEM_LIMIT_BYTES)


def _cast_specs(jobs, step_of):
    in_specs, out_specs, out_shapes = [], [], []
    for src, layer, n_blocks in jobs:
        _, rows, cols = src.shape
        assert rows % n_blocks == 0 and (rows // n_blocks) % RUN == 0
        blk = rows // n_blocks
        in_specs.append(pl.BlockSpec(
            (1, blk, cols), lambda *ids, _l=layer, _n=n_blocks: (_l, jnp.minimum(step_of(*ids), _n - 1), 0)))
        out_specs.append(pl.BlockSpec(
            (blk, cols), lambda *ids, _n=n_blocks: (jnp.minimum(step_of(*ids), _n - 1), 0)))
        out_shapes.append(jax.ShapeDtypeStruct((rows, cols), BF16))
    return in_specs, out_specs, out_shapes


def _with_casts(body, n_in, n_out, n_cast):
    def kernel(*refs):
        ins, refs = refs[:n_in], refs[n_in:]
        cast_src, refs = refs[:n_cast], refs[n_cast:]
        outs, refs = refs[:n_out], refs[n_out:]
        cast_dst, scratch = refs[:n_cast], refs[n_cast:]

        def side_work():
            for src, dst in zip(cast_src, cast_dst):
                dst[...] = src[0].astype(BF16)

        body(*ins, *outs, *scratch, side_work=side_work)
    return kernel


def _layer_norm(x, g, b):
    mu = jnp.mean(x, axis=-1, keepdims=True)
    xc = x - mu
    var = jnp.mean(xc * xc, axis=-1, keepdims=True)
    return xc * lax.rsqrt(var + LN_EPS) * g + b


def _perm_kernel(x_ref, o_ref, slab_ref):
    n_slabs, rows, _ = slab_ref.shape
    for c in range(n_slabs):
        slab_ref[c] = x_ref[:, c * LANES:(c + 1) * LANES]

    def one_superblock(sb, carry):
        base = pl.multiple_of(sb * SUPER, SUPER)
        for r in range(RUN):
            for c in range(n_slabs):
                o_ref[pl.ds(base + r * RUN, RUN), c * LANES:(c + 1) * LANES] = (
                    slab_ref[c, pl.ds(base + r, RUN, stride=RUN), :])
        return carry

    lax.fori_loop(0, rows // SUPER, one_superblock, 0)


def _perm(x):
    t, d = x.shape
    return pl.pallas_call(
        _perm_kernel,
        grid=(t // PERM_ROWS,),
        in_specs=[pl.BlockSpec((PERM_ROWS, d), lambda i: (i, 0))],
        out_specs=pl.BlockSpec((PERM_ROWS, d), lambda i: (i, 0)),
        out_shape=jax.ShapeDtypeStruct((t, d), x.dtype),
        scratch_shapes=[pltpu.VMEM((d // LANES, PERM_ROWS, LANES), x.dtype)],
        compiler_params=_params("parallel"),
        name="perm",
    )(x)


def _memkv_kernel(mem_ref, w_ref, o_ref, *, side_work):
    side_work()
    o_ref[0, 0] = jnp.dot(mem_ref[0].astype(BF16), w_ref[0].astype(BF16),
                          preferred_element_type=F32).astype(BF16)


def _memkv(mem, w_mem_kv, cast_jobs=()):
    nb, m, d = mem.shape
    depth, _, n = w_mem_kv.shape
    c_in, c_out, c_shape = _cast_specs(cast_jobs, lambda i, b: i * nb + b)
    kv, *casts = pl.pallas_call(
        _with_casts(_memkv_kernel, 2, 1, len(cast_jobs)),
        grid=(depth, nb),
        in_specs=[pl.BlockSpec((1, m, d), lambda i, b: (b, 0, 0)),
                  pl.BlockSpec((1, d, n), lambda i, b: (i, 0, 0))] + c_in,
        out_specs=[pl.BlockSpec((1, 1, m, n), lambda i, b: (i, b, 0, 0))] + c_out,
        out_shape=[jax.ShapeDtypeStruct((depth, nb, m, n), BF16)] + c_shape,
        compiler_params=_params("arbitrary", "arbitrary"),
        name="memkv",
    )(mem, w_mem_kv, *[job[0] for job in cast_jobs])
    return kv, casts


def _proj_kernel(x_ref, w_ref, o_ref, xb_ref, *, gelu, side_work):
    @pl.when(pl.program_id(1) == 0)
    def _():
        xb_ref[...] = x_ref[...].astype(BF16)

    side_work()
    acc = jnp.dot(xb_ref[...], w_ref[...], preferred_element_type=F32)
    if gelu:
        acc = 0.5 * acc * (1.0 + lax.erf(acc * SQRT_HALF))
    o_ref[...] = acc.astype(o_ref.dtype)


def _proj(x, w, n_out, gelu, cast_jobs=()):
    t, d = x.shape
    n_j = n_out // PROJ_TN
    c_in, c_out, c_shape = _cast_specs(cast_jobs, lambda i, j: i * n_j + j)
    out, *casts = pl.pallas_call(
        _with_casts(functools.partial(_proj_kernel, gelu=gelu), 2, 1, len(cast_jobs)),
        grid=(t // PROJ_TM, n_j),
        in_specs=[pl.BlockSpec((PROJ_TM, d), lambda i, j: (i, 0)),
                  pl.BlockSpec((d, PROJ_TN), lambda i, j: (0, j))] + c_in,
        out_specs=[pl.BlockSpec((PROJ_TM, PROJ_TN), lambda i, j: (i, j))] + c_out,
        out_shape=[jax.ShapeDtypeStruct((t, n_out), BF16)] + c_shape,
        scratch_shapes=[pltpu.VMEM((PROJ_TM, d), BF16)],
        compiler_params=_params("arbitrary", "arbitrary"),
        name="proj_gelu" if gelu else "proj",
    )(x, w, *[job[0] for job in cast_jobs])
    return out, casts


def _class_pos(i, dilation):
    per_super = SUPER // dilation
    n_a = RUN // dilation
    return (i & ~(per_super - 1)) + n_a * (i & (RUN - 1)) + ((i >> 4) & (n_a - 1))


def _attn_kernel(q_ref, kc_ref, kp_ref, vc_ref, vp_ref, o_ref, lse_ref,
                 qbuf, kbuf, vbuf, obuf, lbuf, bias_ref, *, rows, unit, tile, heads, dilation):
    n_keys = 2 * unit
    width = heads * HEAD_DIM
    chunk = pl.program_id(2)
    first_step = (pl.program_id(0) == 0) & (pl.program_id(1) == 0) & (chunk == 0)

    @pl.when(first_step)
    def _():
        neg_inf = jnp.float32(-jnp.inf)
        for qt in range(unit // tile):
            row = lax.broadcasted_iota(jnp.int32, (tile, n_keys), 0) + qt * tile
            col = lax.broadcasted_iota(jnp.int32, (tile, n_keys), 1)
            dist = unit + _class_pos(row, dilation) - (_class_pos(col & (unit - 1), dilation) + (col & unit))
            band = (dist >= 0) & (dist <= A_WINDOW)
            bias_ref[0, qt] = jnp.where(band & (col >= unit), 0.0, neg_inf)
            bias_ref[1, qt] = jnp.where(band, 0.0, neg_inf)

    qbuf[...] = q_ref[...].reshape(rows, width)
    kbuf[0:unit, :] = kp_ref[...].reshape(unit, width)
    kbuf[unit:, :] = kc_ref[...].reshape(rows, width)
    vbuf[0:unit, :] = vp_ref[...].reshape(unit, width)
    vbuf[unit:, :] = vc_ref[...].reshape(rows, width)

    has_prev = jnp.where(chunk > 0, 1, 0)
    lane = lax.broadcasted_iota(jnp.int32, (tile, LANES), 1)
    for u in range(rows // unit):
        k0 = u * unit
        for qt in range(unit // tile):
            bias = bias_ref[has_prev if u == 0 else 1, qt]
            r0 = k0 + qt * tile
            lse_tile = jnp.zeros((tile, LANES), F32)
            for h in range(heads):
                cs = slice(h * HEAD_DIM, (h + 1) * HEAD_DIM)
                s = lax.dot_general(qbuf[r0:r0 + tile, cs], kbuf[k0:k0 + n_keys, cs], _NT,
                                    preferred_element_type=F32) * SCALE + bias
                m = jnp.max(s, axis=-1, keepdims=True)
                p = jnp.exp(s - m)
                l = jnp.sum(p, axis=-1, keepdims=True)
                o = jnp.dot(p.astype(BF16), vbuf[k0:k0 + n_keys, cs], preferred_element_type=F32) / l
                obuf[r0:r0 + tile, cs] = o.astype(BF16)
                lse_tile = jnp.where(lane == h, m + jnp.log(l), lse_tile)
            lbuf[r0:r0 + tile, :] = lse_tile
    o_ref[...] = obuf[...].reshape(o_ref.shape)
    lse_ref[...] = lbuf[...].reshape(lse_ref.shape)


def _attn_group(qkv, batch, seq, group, window, dilation):
    t, cols = qkv.shape
    width = cols // (A_GROUPS * 3)
    heads = width // HEAD_DIM
    assert window // dilation == A_WINDOW and RUN % dilation == 0
    n_super = seq // SUPER
    n_a = RUN // dilation
    per_super = SUPER // dilation
    unit = max(A_WINDOW, per_super)
    rows = min(ATTN_ROWS, seq // dilation)
    assert rows % unit == 0 and unit % per_super == 0
    tile = unit

    def view(a, last):
        return a.reshape(batch, n_super, n_a, dilation, RUN, last)

    def cur(last, col_block):
        return pl.BlockSpec((None, rows // per_super, n_a, None, RUN, last),
                            lambda b, r, c: (b, c, 0, r, 0, col_block))

    def prev(col_block):
        return pl.BlockSpec((None, unit // per_super, n_a, None, RUN, width),
                            lambda b, r, c: (b, jnp.maximum(c * (rows // unit) - 1, 0), 0, r, 0, col_block))

    qkv_v = view(qkv, cols)
    q_col, k_col, v_col = (group * 3 + i for i in range(3))
    o, lse = pl.pallas_call(
        functools.partial(_attn_kernel, rows=rows, unit=unit, tile=tile, heads=heads, dilation=dilation),
        grid=(batch, dilation, (seq // dilation) // rows),
        in_specs=[cur(width, q_col), cur(width, k_col), prev(k_col), cur(width, v_col), prev(v_col)],
        out_specs=[cur(width, 0), cur(LANES, 0)],
        out_shape=[jax.ShapeDtypeStruct((batch, n_super, n_a, dilation, RUN, width), BF16),
                   jax.ShapeDtypeStruct((batch, n_super, n_a, dilation, RUN, LANES), F32)],
        scratch_shapes=[pltpu.VMEM((rows, width), BF16),
                        pltpu.VMEM((rows + unit, width), BF16),
                        pltpu.VMEM((rows + unit, width), BF16),
                        pltpu.VMEM((rows, width), BF16),
                        pltpu.VMEM((rows, LANES), F32),
                        pltpu.VMEM((2, unit // tile, tile, 2 * unit), F32)],
        compiler_params=_params("arbitrary", "arbitrary", "arbitrary"),
        name=f"attn_d{dilation}",
    )(qkv_v, qkv_v, qkv_v, qkv_v, qkv_v)
    return o.reshape(t, width), lse.reshape(t, LANES)


def _mem_attn_outproj_ln(rs, x_ref, wmq_ref, kv_ref, wout_ref, g_ref, b_ref, out_ref, outb_ref, cat_ref,
                         mix_width):
    xf = x_ref[rs, :]
    y = jnp.dot(cat_ref[rs, 0:mix_width], wout_ref[0:mix_width, :], preferred_element_type=F32)
    mq = jnp.dot(xf.astype(BF16), wmq_ref[...], preferred_element_type=F32).astype(BF16)
    for h in range(MEM_HEADS):
        c0 = h * HEAD_DIM
        k = kv_ref[0, 0, :, c0:c0 + HEAD_DIM]
        v = kv_ref[0, 0, :, MEM_WIDTH + c0:MEM_WIDTH + c0 + HEAD_DIM]
        s = lax.dot_general(mq[:, c0:c0 + HEAD_DIM], k, _NT, preferred_element_type=F32) * SCALE
        p = jnp.exp(s - jnp.max(s, axis=-1, keepdims=True))
        l = jnp.sum(p, axis=-1, keepdims=True)
        o = jnp.dot(p.astype(BF16), v, preferred_element_type=F32) / l
        cat_ref[rs, mix_width + c0:mix_width + c0 + HEAD_DIM] = o.astype(BF16)
    y = y + jnp.dot(cat_ref[rs, mix_width:], wout_ref[mix_width:, :], preferred_element_type=F32)
    x1 = _layer_norm(ALPHA * xf + y, g_ref[0], b_ref[0])
    out_ref[rs, :] = x1
    outb_ref[rs, :] = x1.astype(BF16)


def _row_blocks(tm):
    return [slice(r, r + MIX_SUB) for r in range(0, tm, MIX_SUB)]


def _mix_a_kernel(o0_ref, o1_ref, o2_ref, l0_ref, l1_ref, l2_ref, x_ref, wmq_ref, kv_ref, wout_ref,
                  g_ref, b_ref, out_ref, outb_ref, cat_ref, *, heads):
    for rs in _row_blocks(x_ref.shape[0]):
        l0, l1, l2 = l0_ref[rs, :], l1_ref[rs, :], l2_ref[rs, :]
        mx = jnp.maximum(jnp.maximum(l0, l1), l2)
        e0, e1, e2 = jnp.exp(l0 - mx), jnp.exp(l1 - mx), jnp.exp(l2 - mx)
        den = e0 + e1 + e2
        w0, w1 = e0 / den, e1 / den
        for h in range(heads):
            sl = slice(h * HEAD_DIM, (h + 1) * HEAD_DIM)
            o2 = o2_ref[rs, sl].astype(F32)
            mix = (o2 + w0[:, h:h + 1] * (o0_ref[rs, sl].astype(F32) - o2)
                   + w1[:, h:h + 1] * (o1_ref[rs, sl].astype(F32) - o2))
            cat_ref[rs, sl] = mix.astype(BF16)
        _mem_attn_outproj_ln(rs, x_ref, wmq_ref, kv_ref, wout_ref, g_ref, b_ref, out_ref, outb_ref, cat_ref,
                             heads * HEAD_DIM)


def _mix_b_kernel(u_ref, v_ref, ws_ref, bs_ref, vg_ref, vb_ref, x_ref, wmq_ref, kv_ref, wout_ref,
                  g_ref, b_ref, out_ref, outb_ref, cat_ref, vn_ref, wsp_ref, *, groups):
    tm = u_ref.shape[0]

    @pl.when(pl.program_id(0) == 0)
    def _():
        row = lax.broadcasted_iota(jnp.int32, (SUPER, SUPER), 0)
        col = lax.broadcasted_iota(jnp.int32, (SUPER, SUPER), 1)
        nat_row = RUN * (row & (RUN - 1)) + (row >> 4)
        nat_col = RUN * (col & (RUN - 1)) + (col >> 4)
        same_chunk = (nat_row // B_CHUNK) == (nat_col // B_CHUNK)
        causal = (same_chunk & (nat_col <= nat_row)).astype(F32)
        i = lax.broadcasted_iota(jnp.int32, (SUPER, B_CHUNK), 0)
        p = lax.broadcasted_iota(jnp.int32, (SUPER, B_CHUNK), 1)
        pos = (RUN * (i & (RUN - 1)) + (i >> 4)) % B_CHUNK
        pick = (p == pos).astype(BF16)
        for g in range(groups):
            rows = jnp.dot(pick, ws_ref[0, g].astype(BF16), preferred_element_type=F32).astype(BF16)
            full = lax.dot_general(rows, pick, _NT, preferred_element_type=F32)
            wsp_ref[g] = (full * causal).astype(BF16)

    for rs in _row_blocks(tm):
        vn_ref[rs, :] = _layer_norm(v_ref[rs, :].astype(F32), vg_ref[0], vb_ref[0]).astype(BF16)
        for g in range(groups):
            sl = slice(g * HEAD_DIM, (g + 1) * HEAD_DIM)
            for s0 in range(rs.start, rs.stop, SUPER):
                sb = slice(s0, s0 + SUPER)
                sg = jnp.dot(wsp_ref[g], vn_ref[sb, sl], preferred_element_type=F32) + bs_ref[0, :, g:g + 1]
                cat_ref[sb, sl] = (u_ref[sb, sl].astype(F32) * sg).astype(BF16)
        _mem_attn_outproj_ln(rs, x_ref, wmq_ref, kv_ref, wout_ref, g_ref, b_ref, out_ref, outb_ref, cat_ref,
                             groups * HEAD_DIM)


def _mix_tail_in_specs(d, seq, w_in_shape, layer, kv_shape, wout_shape):
    tiles_per_batch = seq // MIX_TM
    _, _, m, kvw = kv_shape
    k_out, _ = wout_shape
    mq_block = (w_in_shape[1] - MEM_WIDTH) // MEM_WIDTH
    return [
        pl.BlockSpec((MIX_TM, d), lambda i: (i, 0)),
        pl.BlockSpec((d, MEM_WIDTH), lambda i: (0, mq_block)),
        pl.BlockSpec((1, 1, m, kvw), lambda i: (layer, i // tiles_per_batch, 0, 0)),
        pl.BlockSpec((k_out, d), lambda i: (0, 0)),
        pl.BlockSpec((1, 1, d), lambda i: (layer, 0, 0)),
        pl.BlockSpec((1, 1, d), lambda i: (layer, 0, 0)),
    ]


def _mix_a(x, outs, lses, seq, w_in, kv, w_out, ln_g, ln_b, layer):
    t, d = x.shape
    width = outs[0].shape[1]
    tail = _mix_tail_in_specs(d, seq, w_in.shape, layer, kv.shape, w_out.shape)
    return pl.pallas_call(
        functools.partial(_mix_a_kernel, heads=width // HEAD_DIM),
        grid=(t // MIX_TM,),
        in_specs=[pl.BlockSpec((MIX_TM, width), lambda i: (i, 0))] * 3
                 + [pl.BlockSpec((MIX_TM, LANES), lambda i: (i, 0))] * 3 + tail,
        out_specs=[pl.BlockSpec((MIX_TM, d), lambda i: (i, 0))] * 2,
        out_shape=[jax.ShapeDtypeStruct((t, d), F32), jax.ShapeDtypeStruct((t, d), BF16)],
        scratch_shapes=[pltpu.VMEM((MIX_TM, w_out.shape[0]), BF16)],
        compiler_params=_params("parallel"),
        name="mix_a",
    )(*outs, *lses, x, w_in, kv, w_out, ln_g, ln_b)


def _mix_b(x, z, seq, w_s, bs_perm, vn_g, vn_b, layer_in, w_in, kv, w_out, ln_g, ln_b, layer):
    t, d = x.shape
    width = z.shape[1] // 2
    groups = width // HEAD_DIM
    tail = _mix_tail_in_specs(d, seq, w_in.shape, layer, kv.shape, w_out.shape)
    return pl.pallas_call(
        functools.partial(_mix_b_kernel, groups=groups),
        grid=(t // MIX_TM,),
        in_specs=[pl.BlockSpec((MIX_TM, width), lambda i: (i, 0)),
                  pl.BlockSpec((MIX_TM, width), lambda i: (i, 1)),
                  pl.BlockSpec((1, groups, B_CHUNK, B_CHUNK), lambda i: (layer_in, 0, 0, 0)),
                  pl.BlockSpec((1, SUPER, groups), lambda i: (layer_in, 0, 0)),
                  pl.BlockSpec((1, 1, width), lambda i: (layer_in, 0, 0)),
                  pl.BlockSpec((1, 1, width), lambda i: (layer_in, 0, 0))] + tail,
        out_specs=[pl.BlockSpec((MIX_TM, d), lambda i: (i, 0))] * 2,
        out_shape=[jax.ShapeDtypeStruct((t, d), F32), jax.ShapeDtypeStruct((t, d), BF16)],
        scratch_shapes=[pltpu.VMEM((MIX_TM, w_out.shape[0]), BF16),
                        pltpu.VMEM((MIX_TM, width), BF16),
                        pltpu.VMEM((groups, SUPER, SUPER), BF16)],
        compiler_params=_params("arbitrary"),
        name="mix_b",
    )(z, z, w_s, bs_perm, vn_g, vn_b, x, w_in, kv, w_out, ln_g, ln_b)


def _ffn_kernel(x_ref, xb_ref, w1_ref, w2_ref, g_ref, b_ref, o_ref, acc0_ref, acc1_ref, stage_ref, *,
                natural_out, side_work):
    i, j = pl.program_id(0), pl.program_id(1)
    n_tiles = pl.num_programs(0) - 1
    rows = x_ref.shape[0]

    @pl.when((i == 0) & (j == 0))
    def _():
        acc0_ref[...] = jnp.zeros_like(acc0_ref)
        acc1_ref[...] = jnp.zeros_like(acc1_ref)

    def finish_rows(acc_prev):
        sl = pl.ds(pl.multiple_of(j * rows, rows), rows)
        done = _layer_norm(ALPHA * x_ref[...] + acc_prev[sl, :], g_ref[0], b_ref[0])
        if natural_out:
            n_slabs = stage_ref.shape[0]
            for c in range(n_slabs):
                stage_ref[c] = done[:, c * LANES:(c + 1) * LANES]
            for lo in range(RUN):
                for c in range(n_slabs):
                    o_ref[lo, :, c * LANES:(c + 1) * LANES] = stage_ref[c, pl.ds(lo, rows // RUN, stride=RUN), :]
        else:
            o_ref[...] = done
        acc_prev[sl, :] = jnp.zeros((rows, acc_prev.shape[1]), F32)
        return done

    def tile_step(acc_cur, acc_prev):
        side_work()
        done = finish_rows(acc_prev)
        word = pltpu.bitcast(jnp.max(done, axis=0, keepdims=True)[:, 0:LANES], jnp.uint32)
        zero = ((word >> 16) >> 16).astype(F32)
        h = jnp.maximum(jnp.dot(xb_ref[...], w1_ref[...], preferred_element_type=F32), 0.0)
        h = h * h + zero[:, 0:1]
        acc_cur[...] += jnp.dot(h.astype(BF16), w2_ref[...], preferred_element_type=F32)

    @pl.when((i < n_tiles) & (lax.rem(i, 2) == 0))
    def _():
        tile_step(acc0_ref, acc1_ref)

    @pl.when((i < n_tiles) & (lax.rem(i, 2) == 1))
    def _():
        tile_step(acc1_ref, acc0_ref)

    @pl.when((i == n_tiles) & (lax.rem(i, 2) == 0))
    def _():
        finish_rows(acc1_ref)

    @pl.when((i == n_tiles) & (lax.rem(i, 2) == 1))
    def _():
        finish_rows(acc0_ref)


def _ffn(x, xb, w1, w2, ln_g, ln_b, layer, cast_jobs=(), natural_out=False):
    t, d = x.shape
    d_ff = w1.shape[1]
    n_i, n_j = t // FFN_TM, d_ff // FFN_TF
    rows = FFN_TM // n_j
    assert rows % RUN == 0 and SUPER % rows == 0
    per_super = SUPER // rows
    run = rows // RUN

    def finished(i, j):
        return (jnp.where(i == 0, 0, (i - 1) * n_j + j), 0)

    def w_tile(i, j):
        return jnp.where(i == n_i, n_j - 1, j)

    if natural_out:
        def finished_natural(i, j):
            k = finished(i, j)[0]
            return (k // per_super, 0, k % per_super, 0, 0)
        out_spec = pl.BlockSpec((None, RUN, None, run, d), finished_natural)
        out_shape = jax.ShapeDtypeStruct((t // SUPER, RUN, per_super, run, d), F32)
    else:
        out_spec = pl.BlockSpec((rows, d), finished)
        out_shape = jax.ShapeDtypeStruct((t, d), F32)

    c_in, c_out, c_shape = _cast_specs(cast_jobs, lambda i, j: i * n_j + j)
    out, *casts = pl.pallas_call(
        _with_casts(functools.partial(_ffn_kernel, natural_out=natural_out), 6, 1, len(cast_jobs)),
        grid=(n_i + 1, n_j),
        in_specs=[pl.BlockSpec((rows, d), finished),
                  pl.BlockSpec((FFN_TM, d), lambda i, j: (jnp.minimum(i, n_i - 1), 0)),
                  pl.BlockSpec((d, FFN_TF), lambda i, j: (0, w_tile(i, j))),
                  pl.BlockSpec((FFN_TF, d), lambda i, j: (w_tile(i, j), 0)),
                  pl.BlockSpec((1, 1, d), lambda i, j: (layer, 0, 0)),
                  pl.BlockSpec((1, 1, d), lambda i, j: (layer, 0, 0))] + c_in,
        out_specs=[out_spec] + c_out,
        out_shape=[out_shape] + c_shape,
        scratch_shapes=[pltpu.VMEM((FFN_TM, d), F32), pltpu.VMEM((FFN_TM, d), F32),
                        pltpu.VMEM((d // LANES, rows, LANES), F32)],
        compiler_params=_params("arbitrary", "arbitrary"),
        name="ffn",
    )(x, xb, w1, w2, ln_g, ln_b, *[job[0] for job in cast_jobs])
    return out.reshape(t, d), casts


def _gmlp_bias_in_working_order(b_s):
    n_layers, groups, chunk = b_s.shape
    half = chunk // RUN
    reps = SUPER // chunk
    b = b_s.reshape(n_layers, groups, half, RUN).transpose(0, 3, 2, 1)
    b = jnp.broadcast_to(b[:, :, None], (n_layers, RUN, reps, half, groups))
    return b.reshape(n_layers, SUPER, groups)


def kernel(x, mem, w_in_a, w_in_b, w_s, b_s, vnorm_g, vnorm_b, w_mem_kv, w_out, ln1_g, ln1_b, w_ff1, w_ff2,
           ln2_g, ln2_b):
    batch, seq, d = x.shape
    row3 = lambda a: a.reshape(a.shape[0], 1, a.shape[1])
    ln1_g, ln1_b, ln2_g, ln2_b = row3(ln1_g), row3(ln1_b), row3(ln2_g), row3(ln2_b)
    vnorm_g, vnorm_b = row3(vnorm_g), row3(vnorm_b)
    bs_perm = _gmlp_bias_in_working_order(b_s)
    a_cols = w_in_a.shape[2] - MEM_WIDTH
    b_cols = w_in_b.shape[2] - MEM_WIDTH

    def cast_job(kind, layer, n_blocks):
        if kind == "in":
            return (w_in_a if layer % 2 == 0 else w_in_b, layer // 2, n_blocks)
        return ({"out": w_out, "ff1": w_ff1, "ff2": w_ff2}[kind], layer, n_blocks)

    hosted = {("memkv", 0): [("in", 0, 8)],
              ("proj", 0): [("out", 0, 48), ("ff1", 0, 64), ("ff2", 0, 64), ("in", 1, 64), ("out", 1, 48),
                            ("in", 2, 64)],
              ("ffn", 0): [("ff1", 1, 64), ("ff2", 1, 64)],
              ("ffn", 1): [("ff1", 2, 64), ("ff2", 2, 64)],
              ("proj", 2): [("out", 2, 48), ("out", 3, 48), ("in", 3, 64)],
              ("ffn", 2): [("ff1", 3, 64), ("ff2", 3, 64)]}
    bf16 = {}

    def jobs_of(host):
        return [cast_job(*job) for job in hosted.get(host, [])]

    def collect(host, casts):
        bf16.update(zip([job[:2] for job in hosted.get(host, [])], casts))

    xf = _perm(x.reshape(batch * seq, d))
    kv, casts = _memkv(mem, w_mem_kv, cast_jobs=jobs_of(("memkv", 0)))
    collect(("memkv", 0), casts)
    for i in range(DEPTH):
        w_in = bf16["in", i]
        if i % 2 == 0:
            qkv, casts = _proj(xf, w_in, a_cols, gelu=False, cast_jobs=jobs_of(("proj", i)))
            collect(("proj", i), casts)
            outs, lses = [], []
            for g, (window, dilation) in enumerate(A_PAIRS):
                o, lse = _attn_group(qkv, batch, seq, g, window, dilation)
                outs.append(o)
                lses.append(lse)
            xf, xb = _mix_a(xf, outs, lses, seq, w_in, kv, bf16["out", i], ln1_g, ln1_b, i)
        else:
            z, _ = _proj(xf, w_in, b_cols, gelu=True)
            xf, xb = _mix_b(xf, z, seq, w_s, bs_perm, vnorm_g, vnorm_b, i // 2, w_in, kv, bf16["out", i],
                            ln1_g, ln1_b, i)
        xf, casts = _ffn(xf, xb, bf16["ff1", i], bf16["ff2", i], ln2_g, ln2_b, i, cast_jobs=jobs_of(("ffn", i)),
                         natural_out=(i == DEPTH - 1))
        collect(("ffn", i), casts)
    return xf.reshape(batch, seq, d)
```

```python
import functools

import numpy as np
import jax
import jax.numpy as jnp
from jax import lax
from jax.experimental import pallas as pl
from jax.experimental.pallas import tpu as pltpu

F32 = jnp.float32
BF16 = jnp.bfloat16

HEAD_DIM = 128
A_PAIRS = ((128, 1), (512, 4), (2048, 16))
A_GROUPS = len(A_PAIRS)
A_WINDOW = 128
B_CHUNK = 128
MEM_HEADS = 4
MEM_WIDTH = MEM_HEADS * HEAD_DIM
DEPTH = 4
ALPHA = (2.0 * DEPTH) ** 0.25
LN_EPS = 1e-5
SCALE = HEAD_DIM ** -0.5
SQRT_HALF = np.float32(np.sqrt(0.5))

LANES = 128
VMEM_LIMIT_BYTES = 56 * 1024 * 1024

RUN = 16
SUPER = RUN * RUN

PERM_ROWS = 4 * SUPER
PROJ_TM, PROJ_TN = 1024, 1024
ATTN_ROWS = 1024
MIX_TM = 512
MIX_SUB = MIX_TM
FFN_TM, FFN_TF = 1024, 1024

_NT = (((1,), (1,)), ((), ()))


def _params(*sem):
    return pltpu.CompilerParams(dimension_semantics=sem, vmem_limit_bytes=VMEM_LIMIT_BYTES)


def _cast_specs(jobs, step_of):
    in_specs, out_specs, out_shapes = [], [], []
    for src, layer, n_blocks in jobs:
        _, rows, cols = src.shape
        assert rows % n_blocks == 0 and (rows // n_blocks) % RUN == 0
        blk = rows // n_blocks
        in_specs.append(pl.BlockSpec(
            (1, blk, cols), lambda *ids, _l=layer, _n=n_blocks: (_l, jnp.minimum(step_of(*ids), _n - 1), 0)))
        out_specs.append(pl.BlockSpec(
            (blk, cols), lambda *ids, _n=n_blocks: (jnp.minimum(step_of(*ids), _n - 1), 0)))
        out_shapes.append(jax.ShapeDtypeStruct((rows, cols), BF16))
    return in_specs, out_specs, out_shapes


def _with_casts(body, n_in, n_out, n_cast):
    def kernel(*refs):
        ins, refs = refs[:n_in], refs[n_in:]
        cast_src, refs = refs[:n_cast], refs[n_cast:]
        outs, refs = refs[:n_out], refs[n_out:]
        cast_dst, scratch = refs[:n_cast], refs[n_cast:]

        def side_work():
            for src, dst in zip(cast_src, cast_dst):
                dst[...] = src[0].astype(BF16)

        body(*ins, *outs, *scratch, side_work=side_work)
    return kernel


def _layer_norm(x, g, b):
    mu = jnp.mean(x, axis=-1, keepdims=True)
    xc = x - mu
    var = jnp.mean(xc * xc, axis=-1, keepdims=True)
    return xc * lax.rsqrt(var + LN_EPS) * g + b


def _perm_kernel(x_ref, o_ref, slab_ref):
    n_slabs, rows, _ = slab_ref.shape
    for c in range(n_slabs):
        slab_ref[c] = x_ref[:, c * LANES:(c + 1) * LANES]

    def one_superblock(sb, carry):
        base = pl.multiple_of(sb * SUPER, SUPER)
        for r in range(RUN):
            for c in range(n_slabs):
                o_ref[pl.ds(base + r * RUN, RUN), c * LANES:(c + 1) * LANES] = (
                    slab_ref[c, pl.ds(base + r, RUN, stride=RUN), :])
        return carry

    lax.fori_loop(0, rows // SUPER, one_superblock, 0)


def _perm(x):
    t, d = x.shape
    return pl.pallas_call(
        _perm_kernel,
        grid=(t // PERM_ROWS,),
        in_specs=[pl.BlockSpec((PERM_ROWS, d), lambda i: (i, 0))],
        out_specs=pl.BlockSpec((PERM_ROWS, d), lambda i: (i, 0)),
        out_shape=jax.ShapeDtypeStruct((t, d), x.dtype),
        scratch_shapes=[pltpu.VMEM((d // LANES, PERM_ROWS, LANES), x.dtype)],
        compiler_params=_params("parallel"),
        name="perm",
    )(x)


def _memkv_kernel(mem_ref, w_ref, o_ref, *, side_work):
    side_work()
    o_ref[0, 0] = jnp.dot(mem_ref[0].astype(BF16), w_ref[0].astype(BF16),
                          preferred_element_type=F32).astype(BF16)


def _memkv(mem, w_mem_kv, cast_jobs=()):
    nb, m, d = mem.shape
    depth, _, n = w_mem_kv.shape
    c_in, c_out, c_shape = _cast_specs(cast_jobs, lambda i, b: i * nb + b)
    kv, *casts = pl.pallas_call(
        _with_casts(_memkv_kernel, 2, 1, len(cast_jobs)),
        grid=(depth, nb),
        in_specs=[pl.BlockSpec((1, m, d), lambda i, b: (b, 0, 0)),
                  pl.BlockSpec((1, d, n), lambda i, b: (i, 0, 0))] + c_in,
        out_specs=[pl.BlockSpec((1, 1, m, n), lambda i, b: (i, b, 0, 0))] + c_out,
        out_shape=[jax.ShapeDtypeStruct((depth, nb, m, n), BF16)] + c_shape,
        compiler_params=_params("arbitrary", "arbitrary"),
        name="memkv",
    )(mem, w_mem_kv, *[job[0] for job in cast_jobs])
    return kv, casts


def _proj_kernel(x_ref, w_ref, o_ref, xb_ref, *, gelu, side_work):
    @pl.when(pl.program_id(1) == 0)
    def _():
        xb_ref[...] = x_ref[...].astype(BF16)

    side_work()
    acc = jnp.dot(xb_ref[...], w_ref[...], preferred_element_type=F32)
    if gelu:
        acc = 0.5 * acc * (1.0 + lax.erf(acc * SQRT_HALF))
    o_ref[...] = acc.astype(o_ref.dtype)


def _proj(x, w, n_out, gelu, cast_jobs=()):
    t, d = x.shape
    n_j = n_out // PROJ_TN
    c_in, c_out, c_shape = _cast_specs(cast_jobs, lambda i, j: i * n_j + j)
    out, *casts = pl.pallas_call(
        _with_casts(functools.partial(_proj_kernel, gelu=gelu), 2, 1, len(cast_jobs)),
        grid=(t // PROJ_TM, n_j),
        in_specs=[pl.BlockSpec((PROJ_TM, d), lambda i, j: (i, 0)),
                  pl.BlockSpec((d, PROJ_TN), lambda i, j: (0, j))] + c_in,
        out_specs=[pl.BlockSpec((PROJ_TM, PROJ_TN), lambda i, j: (i, j))] + c_out,
        out_shape=[jax.ShapeDtypeStruct((t, n_out), BF16)] + c_shape,
        scratch_shapes=[pltpu.VMEM((PROJ_TM, d), BF16)],
        compiler_params=_params("arbitrary", "arbitrary"),
        name="proj_gelu" if gelu else "proj",
    )(x, w, *[job[0] for job in cast_jobs])
    return out, casts


def _class_pos(i, dilation):
    per_super = SUPER // dilation
    n_a = RUN // dilation
    return (i & ~(per_super - 1)) + n_a * (i & (RUN - 1)) + ((i >> 4) & (n_a - 1))


def _attn_kernel(q_ref, kc_ref, kp_ref, vc_ref, vp_ref, o_ref, lse_ref,
                 qbuf, kbuf, vbuf, obuf, lbuf, bias_ref, *, rows, unit, tile, heads, dilation):
    n_keys = 2 * unit
    width = heads * HEAD_DIM
    chunk = pl.program_id(2)
    first_step = (pl.program_id(0) == 0) & (pl.program_id(1) == 0) & (chunk == 0)

    @pl.when(first_step)
    def _():
        neg_inf = jnp.float32(-jnp.inf)
        for qt in range(unit // tile):
            row = lax.broadcasted_iota(jnp.int32, (tile, n_keys), 0) + qt * tile
            col = lax.broadcasted_iota(jnp.int32, (tile, n_keys), 1)
            dist = unit + _class_pos(row, dilation) - (_class_pos(col & (unit - 1), dilation) + (col & unit))
            band = (dist >= 0) & (dist <= A_WINDOW)
            bias_ref[0, qt] = jnp.where(band & (col >= unit), 0.0, neg_inf)
            bias_ref[1, qt] = jnp.where(band, 0.0, neg_inf)

    for h in range(heads):
        cs = slice(h * HEAD_DIM, (h + 1) * HEAD_DIM)
        qbuf[h] = q_ref[:, :, :, cs].reshape(rows, HEAD_DIM)
        kbuf[h, 0:unit, :] = kp_ref[:, :, :, cs].reshape(unit, HEAD_DIM)
        kbuf[h, unit:, :] = kc_ref[:, :, :, cs].reshape(rows, HEAD_DIM)
        vbuf[h, 0:unit, :] = vp_ref[:, :, :, cs].reshape(unit, HEAD_DIM)
        vbuf[h, unit:, :] = vc_ref[:, :, :, cs].reshape(rows, HEAD_DIM)

    has_prev = jnp.where(chunk > 0, 1, 0)
    lane = lax.broadcasted_iota(jnp.int32, (tile, LANES), 1)
    for u in range(rows // unit):
        k0 = u * unit
        for qt in range(unit // tile):
            bias = bias_ref[has_prev if u == 0 else 1, qt]
            r0 = k0 + qt * tile
            lse_tile = jnp.zeros((tile, LANES), F32)
            for h in range(heads):
                s = lax.dot_general(qbuf[h, r0:r0 + tile, :], kbuf[h, k0:k0 + n_keys, :], _NT,
                                    preferred_element_type=F32) * SCALE + bias
                m = jnp.max(s, axis=-1, keepdims=True)
                p = jnp.exp(s - m)
                l = jnp.sum(p, axis=-1, keepdims=True)
                o = jnp.dot(p.astype(BF16), vbuf[h, k0:k0 + n_keys, :], preferred_element_type=F32) / l
                obuf[h, r0:r0 + tile, :] = o.astype(BF16)
                lse_tile = jnp.where(lane == h, m + jnp.log(l), lse_tile)
            lbuf[r0:r0 + tile, :] = lse_tile
    for h in range(heads):
        o_ref[:, :, :, h * HEAD_DIM:(h + 1) * HEAD_DIM] = obuf[h].reshape(o_ref.shape[:-1] + (HEAD_DIM,))
    lse_ref[...] = lbuf[...].reshape(lse_ref.shape)


def _attn_group(qkv, batch, seq, group, window, dilation):
    t, cols = qkv.shape
    width = cols // (A_GROUPS * 3)
    heads = width // HEAD_DIM
    assert window // dilation == A_WINDOW and RUN % dilation == 0
    n_super = seq // SUPER
    n_a = RUN // dilation
    per_super = SUPER // dilation
    unit = max(A_WINDOW, per_super)
    rows = min(ATTN_ROWS, seq // dilation)
    assert rows % unit == 0 and unit % per_super == 0
    tile = unit

    def view(a, last):
        return a.reshape(batch, n_super, n_a, dilation, RUN, last)

    def cur(last, col_block):
        return pl.BlockSpec((None, rows // per_super, n_a, None, RUN, last),
                            lambda b, r, c: (b, c, 0, r, 0, col_block))

    def prev(col_block):
        return pl.BlockSpec((None, unit // per_super, n_a, None, RUN, width),
                            lambda b, r, c: (b, jnp.maximum(c * (rows // unit) - 1, 0), 0, r, 0, col_block))

    qkv_v = view(qkv, cols)
    q_col, k_col, v_col = (group * 3 + i for i in range(3))
    o, lse = pl.pallas_call(
        functools.partial(_attn_kernel, rows=rows, unit=unit, tile=tile, heads=heads, dilation=dilation),
        grid=(batch, dilation, (seq // dilation) // rows),
        in_specs=[cur(width, q_col), cur(width, k_col), prev(k_col), cur(width, v_col), prev(v_col)],
        out_specs=[cur(width, 0), cur(LANES, 0)],
        out_shape=[jax.ShapeDtypeStruct((batch, n_super, n_a, dilation, RUN, width), BF16),
                   jax.ShapeDtypeStruct((batch, n_super, n_a, dilation, RUN, LANES), F32)],
        scratch_shapes=[pltpu.VMEM((heads, rows, HEAD_DIM), BF16),
                        pltpu.VMEM((heads, rows + unit, HEAD_DIM), BF16),
                        pltpu.VMEM((heads, rows + unit, HEAD_DIM), BF16),
                        pltpu.VMEM((heads, rows, HEAD_DIM), BF16),
                        pltpu.VMEM((rows, LANES), F32),
                        pltpu.VMEM((2, unit // tile, tile, 2 * unit), F32)],
        compiler_params=_params("arbitrary", "arbitrary", "arbitrary"),
        name=f"attn_d{dilation}",
    )(qkv_v, qkv_v, qkv_v, qkv_v, qkv_v)
    return o.reshape(t, width), lse.reshape(t, LANES)


def _mem_attn_outproj_ln(rs, x_ref, wmq_ref, kv_ref, wout_ref, g_ref, b_ref, out_ref, outb_ref, cat_ref,
                         mix_width):
    xf = x_ref[rs, :]
    y = jnp.dot(cat_ref[rs, 0:mix_width], wout_ref[0:mix_width, :], preferred_element_type=F32)
    mq = jnp.dot(xf.astype(BF16), wmq_ref[...], preferred_element_type=F32).astype(BF16)
    for h in range(MEM_HEADS):
        c0 = h * HEAD_DIM
        k = kv_ref[0, 0, :, c0:c0 + HEAD_DIM]
        v = kv_ref[0, 0, :, MEM_WIDTH + c0:MEM_WIDTH + c0 + HEAD_DIM]
        s = lax.dot_general(mq[:, c0:c0 + HEAD_DIM], k, _NT, preferred_element_type=F32) * SCALE
        p = jnp.exp(s - jnp.max(s, axis=-1, keepdims=True))
        l = jnp.sum(p, axis=-1, keepdims=True)
        o = jnp.dot(p.astype(BF16), v, preferred_element_type=F32) / l
        cat_ref[rs, mix_width + c0:mix_width + c0 + HEAD_DIM] = o.astype(BF16)
    y = y + jnp.dot(cat_ref[rs, mix_width:], wout_ref[mix_width:, :], preferred_element_type=F32)
    x1 = _layer_norm(ALPHA * xf + y, g_ref[0], b_ref[0])
    out_ref[rs, :] = x1
    outb_ref[rs, :] = x1.astype(BF16)


def _row_blocks(tm):
    return [slice(r, r + MIX_SUB) for r in range(0, tm, MIX_SUB)]


def _mix_a_kernel(o0_ref, o1_ref, o2_ref, l0_ref, l1_ref, l2_ref, x_ref, wmq_ref, kv_ref, wout_ref,
                  g_ref, b_ref, out_ref, outb_ref, cat_ref, *, heads):
    for rs in _row_blocks(x_ref.shape[0]):
        l0, l1, l2 = l0_ref[rs, :], l1_ref[rs, :], l2_ref[rs, :]
        mx = jnp.maximum(jnp.maximum(l0, l1), l2)
        e0, e1, e2 = jnp.exp(l0 - mx), jnp.exp(l1 - mx), jnp.exp(l2 - mx)
        den = e0 + e1 + e2
        w0, w1 = e0 / den, e1 / den
        for h in range(heads):
            sl = slice(h * HEAD_DIM, (h + 1) * HEAD_DIM)
            o2 = o2_ref[rs, sl].astype(F32)
            mix = (o2 + w0[:, h:h + 1] * (o0_ref[rs, sl].astype(F32) - o2)
                   + w1[:, h:h + 1] * (o1_ref[rs, sl].astype(F32) - o2))
            cat_ref[rs, sl] = mix.astype(BF16)
        _mem_attn_outproj_ln(rs, x_ref, wmq_ref, kv_ref, wout_ref, g_ref, b_ref, out_ref, outb_ref, cat_ref,
                             heads * HEAD_DIM)


def _mix_b_kernel(u_ref, v_ref, ws_ref, bs_ref, vg_ref, vb_ref, x_ref, wmq_ref, kv_ref, wout_ref,
                  g_ref, b_ref, out_ref, outb_ref, cat_ref, vn_ref, wsp_ref, *, groups):
    tm = u_ref.shape[0]

    @pl.when(pl.program_id(0) == 0)
    def _():
        row = lax.broadcasted_iota(jnp.int32, (SUPER, SUPER), 0)
        col = lax.broadcasted_iota(jnp.int32, (SUPER, SUPER), 1)
        nat_row = RUN * (row & (RUN - 1)) + (row >> 4)
        nat_col = RUN * (col & (RUN - 1)) + (col >> 4)
        same_chunk = (nat_row // B_CHUNK) == (nat_col // B_CHUNK)
        causal = (same_chunk & (nat_col <= nat_row)).astype(F32)
        i = lax.broadcasted_iota(jnp.int32, (SUPER, B_CHUNK), 0)
        p = lax.broadcasted_iota(jnp.int32, (SUPER, B_CHUNK), 1)
        pos = (RUN * (i & (RUN - 1)) + (i >> 4)) % B_CHUNK
        pick = (p == pos).astype(BF16)
        for g in range(groups):
            rows = jnp.dot(pick, ws_ref[0, g].astype(BF16), preferred_element_type=F32).astype(BF16)
            full = lax.dot_general(rows, pick, _NT, preferred_element_type=F32)
            wsp_ref[g] = (full * causal).astype(BF16)

    for rs in _row_blocks(tm):
        vn_ref[rs, :] = _layer_norm(v_ref[rs, :].astype(F32), vg_ref[0], vb_ref[0]).astype(BF16)
        for g in range(groups):
            sl = slice(g * HEAD_DIM, (g + 1) * HEAD_DIM)
            for s0 in range(rs.start, rs.stop, SUPER):
                sb = slice(s0, s0 + SUPER)
                sg = jnp.dot(wsp_ref[g], vn_ref[sb, sl], preferred_element_type=F32) + bs_ref[0, :, g:g + 1]
                cat_ref[sb, sl] = (u_ref[sb, sl].astype(F32) * sg).astype(BF16)
        _mem_attn_outproj_ln(rs, x_ref, wmq_ref, kv_ref, wout_ref, g_ref, b_ref, out_ref, outb_ref, cat_ref,
                             groups * HEAD_DIM)


def _mix_tail_in_specs(d, seq, w_in_shape, layer, kv_shape, wout_shape):
    tiles_per_batch = seq // MIX_TM
    _, _, m, kvw = kv_shape
    k_out, _ = wout_shape
    mq_block = (w_in_shape[1] - MEM_WIDTH) // MEM_WIDTH
    return [
        pl.BlockSpec((MIX_TM, d), lambda i: (i, 0)),
        pl.BlockSpec((d, MEM_WIDTH), lambda i: (0, mq_block)),
        pl.BlockSpec((1, 1, m, kvw), lambda i: (layer, i // tiles_per_batch, 0, 0)),
        pl.BlockSpec((k_out, d), lambda i: (0, 0)),
        pl.BlockSpec((1, 1, d), lambda i: (layer, 0, 0)),
        pl.BlockSpec((1, 1, d), lambda i: (layer, 0, 0)),
    ]


def _mix_a(x, outs, lses, seq, w_in, kv, w_out, ln_g, ln_b, layer):
    t, d = x.shape
    width = outs[0].shape[1]
    tail = _mix_tail_in_specs(d, seq, w_in.shape, layer, kv.shape, w_out.shape)
    return pl.pallas_call(
        functools.partial(_mix_a_kernel, heads=width // HEAD_DIM),
        grid=(t // MIX_TM,),
        in_specs=[pl.BlockSpec((MIX_TM, width), lambda i: (i, 0))] * 3
                 + [pl.BlockSpec((MIX_TM, LANES), lambda i: (i, 0))] * 3 + tail,
        out_specs=[pl.BlockSpec((MIX_TM, d), lambda i: (i, 0))] * 2,
        out_shape=[jax.ShapeDtypeStruct((t, d), F32), jax.ShapeDtypeStruct((t, d), BF16)],
        scratch_shapes=[pltpu.VMEM((MIX_TM, w_out.shape[0]), BF16)],
        compiler_params=_params("parallel"),
        name="mix_a",
    )(*outs, *lses, x, w_in, kv, w_out, ln_g, ln_b)


def _mix_b(x, z, seq, w_s, bs_perm, vn_g, vn_b, layer_in, w_in, kv, w_out, ln_g, ln_b, layer):
    t, d = x.shape
    width = z.shape[1] // 2
    groups = width // HEAD_DIM
    tail = _mix_tail_in_specs(d, seq, w_in.shape, layer, kv.shape, w_out.shape)
    return pl.pallas_call(
        functools.partial(_mix_b_kernel, groups=groups),
        grid=(t // MIX_TM,),
        in_specs=[pl.BlockSpec((MIX_TM, width), lambda i: (i, 0)),
                  pl.BlockSpec((MIX_TM, width), lambda i: (i, 1)),
                  pl.BlockSpec((1, groups, B_CHUNK, B_CHUNK), lambda i: (layer_in, 0, 0, 0)),
                  pl.BlockSpec((1, SUPER, groups), lambda i: (layer_in, 0, 0)),
                  pl.BlockSpec((1, 1, width), lambda i: (layer_in, 0, 0)),
                  pl.BlockSpec((1, 1, width), lambda i: (layer_in, 0, 0))] + tail,
        out_specs=[pl.BlockSpec((MIX_TM, d), lambda i: (i, 0))] * 2,
        out_shape=[jax.ShapeDtypeStruct((t, d), F32), jax.ShapeDtypeStruct((t, d), BF16)],
        scratch_shapes=[pltpu.VMEM((MIX_TM, w_out.shape[0]), BF16),
                        pltpu.VMEM((MIX_TM, width), BF16),
                        pltpu.VMEM((groups, SUPER, SUPER), BF16)],
        compiler_params=_params("arbitrary"),
        name="mix_b",
    )(z, z, w_s, bs_perm, vn_g, vn_b, x, w_in, kv, w_out, ln_g, ln_b)


def _ffn_kernel(x_ref, xb_ref, w1_ref, w2_ref, g_ref, b_ref, o_ref, acc0_ref, acc1_ref, stage_ref, *,
                natural_out, side_work):
    i, j = pl.program_id(0), pl.program_id(1)
    n_tiles = pl.num_programs(0) - 1
    rows = x_ref.shape[0]

    @pl.when((i == 0) & (j == 0))
    def _():
        acc0_ref[...] = jnp.zeros_like(acc0_ref)
        acc1_ref[...] = jnp.zeros_like(acc1_ref)

    def finish_rows(acc_prev):
        sl = pl.ds(pl.multiple_of(j * rows, rows), rows)
        done = _layer_norm(ALPHA * x_ref[...] + acc_prev[sl, :], g_ref[0], b_ref[0])
        if natural_out:
            n_slabs = stage_ref.shape[0]
            for c in range(n_slabs):
                stage_ref[c] = done[:, c * LANES:(c + 1) * LANES]
            for lo in range(RUN):
                for c in range(n_slabs):
                    o_ref[lo, :, c * LANES:(c + 1) * LANES] = stage_ref[c, pl.ds(lo, rows // RUN, stride=RUN), :]
        else:
            o_ref[...] = done
        acc_prev[sl, :] = jnp.zeros((rows, acc_prev.shape[1]), F32)
        return done

    def tile_step(acc_cur, acc_prev):
        side_work()
        done = finish_rows(acc_prev)
        word = pltpu.bitcast(jnp.max(done, axis=0, keepdims=True)[:, 0:LANES], jnp.uint32)
        zero = ((word >> 16) >> 16).astype(F32)
        h = jnp.maximum(jnp.dot(xb_ref[...], w1_ref[...], preferred_element_type=F32), 0.0)
        h = h * h + zero[:, 0:1]
        acc_cur[...] += jnp.dot(h.astype(BF16), w2_ref[...], preferred_element_type=F32)

    @pl.when((i < n_tiles) & (lax.rem(i, 2) == 0))
    def _():
        tile_step(acc0_ref, acc1_ref)

    @pl.when((i < n_tiles) & (lax.rem(i, 2) == 1))
    def _():
        tile_step(acc1_ref, acc0_ref)

    @pl.when((i == n_tiles) & (lax.rem(i, 2) == 0))
    def _():
        finish_rows(acc1_ref)

    @pl.when((i == n_tiles) & (lax.rem(i, 2) == 1))
    def _():
        finish_rows(acc0_ref)


def _ffn(x, xb, w1, w2, ln_g, ln_b, layer, cast_jobs=(), natural_out=False):
    t, d = x.shape
    d_ff = w1.shape[1]
    n_i, n_j = t // FFN_TM, d_ff // FFN_TF
    rows = FFN_TM // n_j
    assert rows % RUN == 0 and SUPER % rows == 0
    per_super = SUPER // rows
    run = rows // RUN

    def finished(i, j):
        return (jnp.where(i == 0, 0, (i - 1) * n_j + j), 0)

    def w_tile(i, j):
        return jnp.where(i == n_i, n_j - 1, j)

    if natural_out:
        def finished_natural(i, j):
            k = finished(i, j)[0]
            return (k // per_super, 0, k % per_super, 0, 0)
        out_spec = pl.BlockSpec((None, RUN, None, run, d), finished_natural)
        out_shape = jax.ShapeDtypeStruct((t // SUPER, RUN, per_super, run, d), F32)
    else:
        out_spec = pl.BlockSpec((rows, d), finished)
        out_shape = jax.ShapeDtypeStruct((t, d), F32)

    c_in, c_out, c_shape = _cast_specs(cast_jobs, lambda i, j: i * n_j + j)
    out, *casts = pl.pallas_call(
        _with_casts(functools.partial(_ffn_kernel, natural_out=natural_out), 6, 1, len(cast_jobs)),
        grid=(n_i + 1, n_j),
        in_specs=[pl.BlockSpec((rows, d), finished),
                  pl.BlockSpec((FFN_TM, d), lambda i, j: (jnp.minimum(i, n_i - 1), 0)),
                  pl.BlockSpec((d, FFN_TF), lambda i, j: (0, w_tile(i, j))),
                  pl.BlockSpec((FFN_TF, d), lambda i, j: (w_tile(i, j), 0)),
                  pl.BlockSpec((1, 1, d), lambda i, j: (layer, 0, 0)),
                  pl.BlockSpec((1, 1, d), lambda i, j: (layer, 0, 0))] + c_in,
        out_specs=[out_spec] + c_out,
        out_shape=[out_shape] + c_shape,
        scratch_shapes=[pltpu.VMEM((FFN_TM, d), F32), pltpu.VMEM((FFN_TM, d), F32),
                        pltpu.VMEM((d // LANES, rows, LANES), F32)],
        compiler_params=_params("arbitrary", "arbitrary"),
        name="ffn",
    )(x, xb, w1, w2, ln_g, ln_b, *[job[0] for job in cast_jobs])
    return out.reshape(t, d), casts


def _gmlp_bias_in_working_order(b_s):
    n_layers, groups, chunk = b_s.shape
    half = chunk // RUN
    reps = SUPER // chunk
    b = b_s.reshape(n_layers, groups, half, RUN).transpose(0, 3, 2, 1)
    b = jnp.broadcast_to(b[:, :, None], (n_layers, RUN, reps, half, groups))
    return b.reshape(n_layers, SUPER, groups)


def kernel(x, mem, w_in_a, w_in_b, w_s, b_s, vnorm_g, vnorm_b, w_mem_kv, w_out, ln1_g, ln1_b, w_ff1, w_ff2,
           ln2_g, ln2_b):
    batch, seq, d = x.shape
    row3 = lambda a: a.reshape(a.shape[0], 1, a.shape[1])
    ln1_g, ln1_b, ln2_g, ln2_b = row3(ln1_g), row3(ln1_b), row3(ln2_g), row3(ln2_b)
    vnorm_g, vnorm_b = row3(vnorm_g), row3(vnorm_b)
    bs_perm = _gmlp_bias_in_working_order(b_s)
    a_cols = w_in_a.shape[2] - MEM_WIDTH
    b_cols = w_in_b.shape[2] - MEM_WIDTH

    def cast_job(kind, layer, n_blocks):
        if kind == "in":
            return (w_in_a if layer % 2 == 0 else w_in_b, layer // 2, n_blocks)
        return ({"out": w_out, "ff1": w_ff1, "ff2": w_ff2}[kind], layer, n_blocks)

    hosted = {("memkv", 0): [("in", 0, 8)],
              ("proj", 0): [("out", 0, 48), ("ff1", 0, 64), ("ff2", 0, 64), ("in", 1, 64), ("out", 1, 48),
                            ("in", 2, 64)],
              ("ffn", 0): [("ff1", 1, 64), ("ff2", 1, 64)],
              ("ffn", 1): [("ff1", 2, 64), ("ff2", 2, 64)],
              ("proj", 2): [("out", 2, 48), ("out", 3, 48), ("in", 3, 64)],
              ("ffn", 2): [("ff1", 3, 64), ("ff2", 3, 64)]}
    bf16 = {}

    def jobs_of(host):
        return [cast_job(*job) for job in hosted.get(host, [])]

    def collect(host, casts):
        bf16.update(zip([job[:2] for job in hosted.get(host, [])], casts))

    xf = _perm(x.reshape(batch * seq, d))
    kv, casts = _memkv(mem, w_mem_kv, cast_jobs=jobs_of(("memkv", 0)))
    collect(("memkv", 0), casts)
    for i in range(DEPTH):
        w_in = bf16["in", i]
        if i % 2 == 0:
            qkv, casts = _proj(xf, w_in, a_cols, gelu=False, cast_jobs=jobs_of(("proj", i)))
            collect(("proj", i), casts)
            outs, lses = [], []
            for g, (window, dilation) in enumerate(A_PAIRS):
                o, lse = _attn_group(qkv, batch, seq, g, window, dilation)
                outs.append(o)
                lses.append(lse)
            xf, xb = _mix_a(xf, outs, lses, seq, w_in, kv, bf16["out", i], ln1_g, ln1_b, i)
        else:
            z, _ = _proj(xf, w_in, b_cols, gelu=True)
            xf, xb = _mix_b(xf, z, seq, w_s, bs_perm, vnorm_g, vnorm_b, i // 2, w_in, kv, bf16["out", i],
                            ln1_g, ln1_b, i)
        xf, casts = _ffn(xf, xb, bf16["ff1", i], bf16["ff2", i], ln2_g, ln2_b, i, cast_jobs=jobs_of(("ffn", i)),
                         natural_out=(i == DEPTH - 1))
        collect(("ffn", i), casts)
    return xf.reshape(batch, seq, d)
```

```python
import functools

import numpy as np
import jax
import jax.numpy as jnp
from jax import lax
from jax.experimental import pallas as pl
from jax.experimental.pallas import tpu as pltpu

F32 = jnp.float32
BF16 = jnp.bfloat16

HEAD_DIM = 128
A_PAIRS = ((128, 1), (512, 4), (2048, 16))
A_GROUPS = len(A_PAIRS)
A_WINDOW = 128
B_CHUNK = 128
MEM_HEADS = 4
MEM_WIDTH = MEM_HEADS * HEAD_DIM
DEPTH = 4
ALPHA = (2.0 * DEPTH) ** 0.25
LN_EPS = 1e-5
SCALE = HEAD_DIM ** -0.5
SQRT_HALF = np.float32(np.sqrt(0.5))

LANES = 128
VMEM_LIMIT_BYTES = 56 * 1024 * 1024

RUN = 16
SUPER = RUN * RUN

PERM_ROWS = 4 * SUPER
PROJ_TM, PROJ_TN = 1024, 1024
ATTN_ROWS = 1024
MIX_TM = 512
MIX_SUB = MIX_TM
FFN_TM, FFN_TF = 1024, 1024

_NT = (((1,), (1,)), ((), ()))


def _params(*sem):
    return pltpu.CompilerParams(dimension_semantics=sem, vmem_limit_bytes=VMEM_LIMIT_BYTES)


def _cast_specs(jobs, step_of):
    in_specs, out_specs, out_shapes = [], [], []
    for src, layer, n_blocks in jobs:
        _, rows, cols = src.shape
        assert rows % n_blocks == 0 and (rows // n_blocks) % RUN == 0
        blk = rows // n_blocks
        in_specs.append(pl.BlockSpec(
            (1, blk, cols), lambda *ids, _l=layer, _n=n_blocks: (_l, jnp.minimum(step_of(*ids), _n - 1), 0)))
        out_specs.append(pl.BlockSpec(
            (blk, cols), lambda *ids, _n=n_blocks: (jnp.minimum(step_of(*ids), _n - 1), 0)))
        out_shapes.append(jax.ShapeDtypeStruct((rows, cols), BF16))
    return in_specs, out_specs, out_shapes


def _with_casts(body, n_in, n_out, n_cast):
    def kernel(*refs):
        ins, refs = refs[:n_in], refs[n_in:]
        cast_src, refs = refs[:n_cast], refs[n_cast:]
        outs, refs = refs[:n_out], refs[n_out:]
        cast_dst, scratch = refs[:n_cast], refs[n_cast:]

        def side_work():
            for src, dst in zip(cast_src, cast_dst):
                dst[...] = src[0].astype(BF16)

        body(*ins, *outs, *scratch, side_work=side_work)
    return kernel


def _layer_norm(x, g, b):
    mu = jnp.mean(x, axis=-1, keepdims=True)
    xc = x - mu
    var = jnp.mean(xc * xc, axis=-1, keepdims=True)
    return xc * lax.rsqrt(var + LN_EPS) * g + b


def _perm_kernel(x_ref, o_ref, slab_ref):
    n_slabs, rows, _ = slab_ref.shape
    for c in range(n_slabs):
        slab_ref[c] = x_ref[:, c * LANES:(c + 1) * LANES]

    def one_superblock(sb, carry):
        base = pl.multiple_of(sb * SUPER, SUPER)
        for r in range(RUN):
            for c in range(n_slabs):
                o_ref[pl.ds(base + r * RUN, RUN), c * LANES:(c + 1) * LANES] = (
                    slab_ref[c, pl.ds(base + r, RUN, stride=RUN), :])
        return carry

    lax.fori_loop(0, rows // SUPER, one_superblock, 0)


def _perm(x):
    t, d = x.shape
    return pl.pallas_call(
        _perm_kernel,
        grid=(t // PERM_ROWS,),
        in_specs=[pl.BlockSpec((PERM_ROWS, d), lambda i: (i, 0))],
        out_specs=pl.BlockSpec((PERM_ROWS, d), lambda i: (i, 0)),
        out_shape=jax.ShapeDtypeStruct((t, d), x.dtype),
        scratch_shapes=[pltpu.VMEM((d // LANES, PERM_ROWS, LANES), x.dtype)],
        compiler_params=_params("parallel"),
        name="perm",
    )(x)


def _memkv_kernel(mem_ref, w_ref, o_ref, kt_ref, *, side_work):
    side_work()
    kv = jnp.dot(mem_ref[0].astype(BF16), w_ref[0].astype(BF16), preferred_element_type=F32)
    o_ref[0, 0] = kv.astype(BF16)
    kt_ref[0, 0] = kv[:, 0:MEM_WIDTH].T.astype(BF16)


def _memkv(mem, w_mem_kv, cast_jobs=()):
    nb, m, d = mem.shape
    depth, _, n = w_mem_kv.shape
    c_in, c_out, c_shape = _cast_specs(cast_jobs, lambda i, b: i * nb + b)
    kv, kt, *casts = pl.pallas_call(
        _with_casts(_memkv_kernel, 2, 2, len(cast_jobs)),
        grid=(depth, nb),
        in_specs=[pl.BlockSpec((1, m, d), lambda i, b: (b, 0, 0)),
                  pl.BlockSpec((1, d, n), lambda i, b: (i, 0, 0))] + c_in,
        out_specs=[pl.BlockSpec((1, 1, m, n), lambda i, b: (i, b, 0, 0)),
                   pl.BlockSpec((1, 1, MEM_WIDTH, m), lambda i, b: (i, b, 0, 0))] + c_out,
        out_shape=[jax.ShapeDtypeStruct((depth, nb, m, n), BF16),
                   jax.ShapeDtypeStruct((depth, nb, MEM_WIDTH, m), BF16)] + c_shape,
        compiler_params=_params("arbitrary", "arbitrary"),
        name="memkv",
    )(mem, w_mem_kv, *[job[0] for job in cast_jobs])
    return kv, kt, casts


def _proj_kernel(x_ref, w_ref, o_ref, xb_ref, *, gelu, side_work):
    @pl.when(pl.program_id(1) == 0)
    def _():
        xb_ref[...] = x_ref[...].astype(BF16)

    side_work()
    acc = jnp.dot(xb_ref[...], w_ref[...], preferred_element_type=F32)
    if gelu:
        acc = 0.5 * acc * (1.0 + lax.erf(acc * SQRT_HALF))
    o_ref[...] = acc.astype(o_ref.dtype)


def _proj(x, w, n_out, gelu, cast_jobs=()):
    t, d = x.shape
    n_j = n_out // PROJ_TN
    c_in, c_out, c_shape = _cast_specs(cast_jobs, lambda i, j: i * n_j + j)
    out, *casts = pl.pallas_call(
        _with_casts(functools.partial(_proj_kernel, gelu=gelu), 2, 1, len(cast_jobs)),
        grid=(t // PROJ_TM, n_j),
        in_specs=[pl.BlockSpec((PROJ_TM, d), lambda i, j: (i, 0)),
                  pl.BlockSpec((d, PROJ_TN), lambda i, j: (0, j))] + c_in,
        out_specs=[pl.BlockSpec((PROJ_TM, PROJ_TN), lambda i, j: (i, j))] + c_out,
        out_shape=[jax.ShapeDtypeStruct((t, n_out), BF16)] + c_shape,
        scratch_shapes=[pltpu.VMEM((PROJ_TM, d), BF16)],
        compiler_params=_params("arbitrary", "arbitrary"),
        name="proj_gelu" if gelu else "proj",
    )(x, w, *[job[0] for job in cast_jobs])
    return out, casts


def _class_pos(i, dilation):
    per_super = SUPER // dilation
    n_a = RUN // dilation
    return (i & ~(per_super - 1)) + n_a * (i & (RUN - 1)) + ((i >> 4) & (n_a - 1))


def _attn_kernel(q_ref, kc_ref, kp_ref, vc_ref, vp_ref, o_ref, lse_ref,
                 qbuf, kbuf, vbuf, obuf, lbuf, bias_ref, *, rows, unit, tile, heads, dilation):
    n_keys = 2 * unit
    width = heads * HEAD_DIM
    chunk = pl.program_id(2)
    first_step = (pl.program_id(0) == 0) & (pl.program_id(1) == 0) & (chunk == 0)

    @pl.when(first_step)
    def _():
        neg_inf = jnp.float32(-jnp.inf)
        for qt in range(unit // tile):
            row = lax.broadcasted_iota(jnp.int32, (tile, n_keys), 0) + qt * tile
            col = lax.broadcasted_iota(jnp.int32, (tile, n_keys), 1)
            dist = unit + _class_pos(row, dilation) - (_class_pos(col & (unit - 1), dilation) + (col & unit))
            band = (dist >= 0) & (dist <= A_WINDOW)
            bias_ref[0, qt] = jnp.where(band & (col >= unit), 0.0, neg_inf)
            bias_ref[1, qt] = jnp.where(band, 0.0, neg_inf)

    for h in range(heads):
        cs = slice(h * HEAD_DIM, (h + 1) * HEAD_DIM)
        qbuf[h] = q_ref[:, :, :, cs].reshape(rows, HEAD_DIM)
        kbuf[h, 0:unit, :] = kp_ref[:, :, :, cs].reshape(unit, HEAD_DIM)
        kbuf[h, unit:, :] = kc_ref[:, :, :, cs].reshape(rows, HEAD_DIM)
        vbuf[h, 0:unit, :] = vp_ref[:, :, :, cs].reshape(unit, HEAD_DIM)
        vbuf[h, unit:, :] = vc_ref[:, :, :, cs].reshape(rows, HEAD_DIM)

    has_prev = jnp.where(chunk > 0, 1, 0)
    lane = lax.broadcasted_iota(jnp.int32, (tile, LANES), 1)
    for u in range(rows // unit):
        k0 = u * unit
        for qt in range(unit // tile):
            bias = bias_ref[has_prev if u == 0 else 1, qt]
            r0 = k0 + qt * tile
            lse_tile = jnp.zeros((tile, LANES), F32)
            for h in range(heads):
                s = lax.dot_general(qbuf[h, r0:r0 + tile, :], kbuf[h, k0:k0 + n_keys, :], _NT,
                                    preferred_element_type=F32) * SCALE + bias
                m = jnp.max(s, axis=-1, keepdims=True)
                p = jnp.exp(s - m)
                l = jnp.sum(p, axis=-1, keepdims=True)
                o = jnp.dot(p.astype(BF16), vbuf[h, k0:k0 + n_keys, :], preferred_element_type=F32) / l
                obuf[h, r0:r0 + tile, :] = o.astype(BF16)
                lse_tile = jnp.where(lane == h, m + jnp.log(l), lse_tile)
            lbuf[r0:r0 + tile, :] = lse_tile
    for h in range(heads):
        o_ref[:, :, :, h * HEAD_DIM:(h + 1) * HEAD_DIM] = obuf[h].reshape(o_ref.shape[:-1] + (HEAD_DIM,))
    lse_ref[...] = lbuf[...].reshape(lse_ref.shape)


def _attn_group(qkv, batch, seq, group, window, dilation):
    t, cols = qkv.shape
    width = cols // (A_GROUPS * 3)
    heads = width // HEAD_DIM
    assert window // dilation == A_WINDOW and RUN % dilation == 0
    n_super = seq // SUPER
    n_a = RUN // dilation
    per_super = SUPER // dilation
    unit = max(A_WINDOW, per_super)
    rows = min(ATTN_ROWS, seq // dilation)
    assert rows % unit == 0 and unit % per_super == 0
    tile = unit

    def view(a, last):
        return a.reshape(batch, n_super, n_a, dilation, RUN, last)

    def cur(last, col_block):
        return pl.BlockSpec((None, rows // per_super, n_a, None, RUN, last),
                            lambda b, r, c: (b, c, 0, r, 0, col_block))

    def prev(col_block):
        return pl.BlockSpec((None, unit // per_super, n_a, None, RUN, width),
                            lambda b, r, c: (b, jnp.maximum(c * (rows // unit) - 1, 0), 0, r, 0, col_block))

    qkv_v = view(qkv, cols)
    q_col, k_col, v_col = (group * 3 + i for i in range(3))
    o, lse = pl.pallas_call(
        functools.partial(_attn_kernel, rows=rows, unit=unit, tile=tile, heads=heads, dilation=dilation),
        grid=(batch, dilation, (seq // dilation) // rows),
        in_specs=[cur(width, q_col), cur(width, k_col), prev(k_col), cur(width, v_col), prev(v_col)],
        out_specs=[cur(width, 0), cur(LANES, 0)],
        out_shape=[jax.ShapeDtypeStruct((batch, n_super, n_a, dilation, RUN, width), BF16),
                   jax.ShapeDtypeStruct((batch, n_super, n_a, dilation, RUN, LANES), F32)],
        scratch_shapes=[pltpu.VMEM((heads, rows, HEAD_DIM), BF16),
                        pltpu.VMEM((heads, rows + unit, HEAD_DIM), BF16),
                        pltpu.VMEM((heads, rows + unit, HEAD_DIM), BF16),
                        pltpu.VMEM((heads, rows, HEAD_DIM), BF16),
                        pltpu.VMEM((rows, LANES), F32),
                        pltpu.VMEM((2, unit // tile, tile, 2 * unit), F32)],
        compiler_params=_params("arbitrary", "arbitrary", "arbitrary"),
        name=f"attn_d{dilation}",
    )(qkv_v, qkv_v, qkv_v, qkv_v, qkv_v)
    return o.reshape(t, width), lse.reshape(t, LANES)


def _mem_attn_outproj_ln(rs, x_ref, wmq_ref, kv_ref, kt_ref, wout_ref, g_ref, b_ref, out_ref, outb_ref, cat_ref,
                         mix_width):
    xf = x_ref[rs, :]
    y = jnp.dot(cat_ref[rs, 0:mix_width], wout_ref[0:mix_width, :], preferred_element_type=F32)
    mq = jnp.dot(xf.astype(BF16), wmq_ref[...], preferred_element_type=F32).astype(BF16)
    for h in range(MEM_HEADS):
        c0 = h * HEAD_DIM
        v = kv_ref[0, 0, :, MEM_WIDTH + c0:MEM_WIDTH + c0 + HEAD_DIM]
        s = jnp.dot(mq[:, c0:c0 + HEAD_DIM], kt_ref[0, 0, c0:c0 + HEAD_DIM, :], preferred_element_type=F32) * SCALE
        p = jnp.exp(s - jnp.max(s, axis=-1, keepdims=True))
        l = jnp.sum(p, axis=-1, keepdims=True)
        o = jnp.dot(p.astype(BF16), v, preferred_element_type=F32) / l
        cat_ref[rs, mix_width + c0:mix_width + c0 + HEAD_DIM] = o.astype(BF16)
    y = y + jnp.dot(cat_ref[rs, mix_width:], wout_ref[mix_width:, :], preferred_element_type=F32)
    x1 = _layer_norm(ALPHA * xf + y, g_ref[0], b_ref[0])
    out_ref[rs, :] = x1
    outb_ref[rs, :] = x1.astype(BF16)


def _row_blocks(tm):
    return [slice(r, r + MIX_SUB) for r in range(0, tm, MIX_SUB)]


def _mix_a_kernel(o0_ref, o1_ref, o2_ref, l0_ref, l1_ref, l2_ref, x_ref, wmq_ref, kv_ref, kt_ref, wout_ref,
                  g_ref, b_ref, out_ref, outb_ref, cat_ref, *, heads):
    for rs in _row_blocks(x_ref.shape[0]):
        l0, l1, l2 = l0_ref[rs, :], l1_ref[rs, :], l2_ref[rs, :]
        mx = jnp.maximum(jnp.maximum(l0, l1), l2)
        e0, e1, e2 = jnp.exp(l0 - mx), jnp.exp(l1 - mx), jnp.exp(l2 - mx)
        den = e0 + e1 + e2
        w0, w1 = e0 / den, e1 / den
        for h in range(heads):
            sl = slice(h * HEAD_DIM, (h + 1) * HEAD_DIM)
            o2 = o2_ref[rs, sl].astype(F32)
            mix = (o2 + w0[:, h:h + 1] * (o0_ref[rs, sl].astype(F32) - o2)
                   + w1[:, h:h + 1] * (o1_ref[rs, sl].astype(F32) - o2))
            cat_ref[rs, sl] = mix.astype(BF16)
        _mem_attn_outproj_ln(rs, x_ref, wmq_ref, kv_ref, kt_ref, wout_ref, g_ref, b_ref, out_ref, outb_ref, cat_ref,
                             heads * HEAD_DIM)


def _mix_b_kernel(u_ref, v_ref, ws_ref, bs_ref, vg_ref, vb_ref, x_ref, wmq_ref, kv_ref, kt_ref, wout_ref,
                  g_ref, b_ref, out_ref, outb_ref, cat_ref, vn_ref, wsp_ref, *, groups):
    tm = u_ref.shape[0]

    @pl.when(pl.program_id(0) == 0)
    def _():
        row = lax.broadcasted_iota(jnp.int32, (SUPER, SUPER), 0)
        col = lax.broadcasted_iota(jnp.int32, (SUPER, SUPER), 1)
        nat_row = RUN * (row & (RUN - 1)) + (row >> 4)
        nat_col = RUN * (col & (RUN - 1)) + (col >> 4)
        same_chunk = (nat_row // B_CHUNK) == (nat_col // B_CHUNK)
        causal = (same_chunk & (nat_col <= nat_row)).astype(F32)
        i = lax.broadcasted_iota(jnp.int32, (SUPER, B_CHUNK), 0)
        p = lax.broadcasted_iota(jnp.int32, (SUPER, B_CHUNK), 1)
        pos = (RUN * (i & (RUN - 1)) + (i >> 4)) % B_CHUNK
        pick = (p == pos).astype(BF16)
        for g in range(groups):
            rows = jnp.dot(pick, ws_ref[0, g].astype(BF16), preferred_element_type=F32).astype(BF16)
            full = lax.dot_general(rows, pick, _NT, preferred_element_type=F32)
            wsp_ref[g] = (full * causal).astype(BF16)

    for rs in _row_blocks(tm):
        vn_ref[rs, :] = _layer_norm(v_ref[rs, :].astype(F32), vg_ref[0], vb_ref[0]).astype(BF16)
        for g in range(groups):
            sl = slice(g * HEAD_DIM, (g + 1) * HEAD_DIM)
            for s0 in range(rs.start, rs.stop, SUPER):
                sb = slice(s0, s0 + SUPER)
                sg = jnp.dot(wsp_ref[g], vn_ref[sb, sl], preferred_element_type=F32) + bs_ref[0, :, g:g + 1]
                cat_ref[sb, sl] = (u_ref[sb, sl].astype(F32) * sg).astype(BF16)
        _mem_attn_outproj_ln(rs, x_ref, wmq_ref, kv_ref, kt_ref, wout_ref, g_ref, b_ref, out_ref, outb_ref, cat_ref,
                             groups * HEAD_DIM)


def _mix_tail_in_specs(d, seq, w_in_shape, layer, kv_shape, wout_shape):
    tiles_per_batch = seq // MIX_TM
    _, _, m, kvw = kv_shape
    k_out, _ = wout_shape
    mq_block = (w_in_shape[1] - MEM_WIDTH) // MEM_WIDTH
    return [
        pl.BlockSpec((MIX_TM, d), lambda i: (i, 0)),
        pl.BlockSpec((d, MEM_WIDTH), lambda i: (0, mq_block)),
        pl.BlockSpec((1, 1, m, kvw), lambda i: (layer, i // tiles_per_batch, 0, 0)),
        pl.BlockSpec((1, 1, MEM_WIDTH, m), lambda i: (layer, i // tiles_per_batch, 0, 0)),
        pl.BlockSpec((k_out, d), lambda i: (0, 0)),
        pl.BlockSpec((1, 1, d), lambda i: (layer, 0, 0)),
        pl.BlockSpec((1, 1, d), lambda i: (layer, 0, 0)),
    ]


def _mix_a(x, outs, lses, seq, w_in, kv, kt, w_out, ln_g, ln_b, layer):
    t, d = x.shape
    width = outs[0].shape[1]
    tail = _mix_tail_in_specs(d, seq, w_in.shape, layer, kv.shape, w_out.shape)
    return pl.pallas_call(
        functools.partial(_mix_a_kernel, heads=width // HEAD_DIM),
        grid=(t // MIX_TM,),
        in_specs=[pl.BlockSpec((MIX_TM, width), lambda i: (i, 0))] * 3
                 + [pl.BlockSpec((MIX_TM, LANES), lambda i: (i, 0))] * 3 + tail,
        out_specs=[pl.BlockSpec((MIX_TM, d), lambda i: (i, 0))] * 2,
        out_shape=[jax.ShapeDtypeStruct((t, d), F32), jax.ShapeDtypeStruct((t, d), BF16)],
        scratch_shapes=[pltpu.VMEM((MIX_TM, w_out.shape[0]), BF16)],
        compiler_params=_params("parallel"),
        name="mix_a",
    )(*outs, *lses, x, w_in, kv, kt, w_out, ln_g, ln_b)


def _mix_b(x, z, seq, w_s, bs_perm, vn_g, vn_b, layer_in, w_in, kv, kt, w_out, ln_g, ln_b, layer):
    t, d = x.shape
    width = z.shape[1] // 2
    groups = width // HEAD_DIM
    tail = _mix_tail_in_specs(d, seq, w_in.shape, layer, kv.shape, w_out.shape)
    return pl.pallas_call(
        functools.partial(_mix_b_kernel, groups=groups),
        grid=(t // MIX_TM,),
        in_specs=[pl.BlockSpec((MIX_TM, width), lambda i: (i, 0)),
                  pl.BlockSpec((MIX_TM, width), lambda i: (i, 1)),
                  pl.BlockSpec((1, groups, B_CHUNK, B_CHUNK), lambda i: (layer_in, 0, 0, 0)),
                  pl.BlockSpec((1, SUPER, groups), lambda i: (layer_in, 0, 0)),
                  pl.BlockSpec((1, 1, width), lambda i: (layer_in, 0, 0)),
                  pl.BlockSpec((1, 1, width), lambda i: (layer_in, 0, 0))] + tail,
        out_specs=[pl.BlockSpec((MIX_TM, d), lambda i: (i, 0))] * 2,
        out_shape=[jax.ShapeDtypeStruct((t, d), F32), jax.ShapeDtypeStruct((t, d), BF16)],
        scratch_shapes=[pltpu.VMEM((MIX_TM, w_out.shape[0]), BF16),
                        pltpu.VMEM((MIX_TM, width), BF16),
                        pltpu.VMEM((groups, SUPER, SUPER), BF16)],
        compiler_params=_params("arbitrary"),
        name="mix_b",
    )(z, z, w_s, bs_perm, vn_g, vn_b, x, w_in, kv, kt, w_out, ln_g, ln_b)


def _ffn_kernel(x_ref, xb_ref, w1_ref, w2_ref, g_ref, b_ref, o_ref, acc0_ref, acc1_ref, stage_ref, *,
                natural_out, side_work):
    i, j = pl.program_id(0), pl.program_id(1)
    n_tiles = pl.num_programs(0) - 1
    rows = x_ref.shape[0]

    @pl.when((i == 0) & (j == 0))
    def _():
        acc0_ref[...] = jnp.zeros_like(acc0_ref)
        acc1_ref[...] = jnp.zeros_like(acc1_ref)

    def finish_rows(acc_prev):
        sl = pl.ds(pl.multiple_of(j * rows, rows), rows)
        done = _layer_norm(ALPHA * x_ref[...] + acc_prev[sl, :], g_ref[0], b_ref[0])
        if natural_out:
            n_slabs = stage_ref.shape[0]
            for c in range(n_slabs):
                stage_ref[c] = done[:, c * LANES:(c + 1) * LANES]
            for lo in range(RUN):
                for c in range(n_slabs):
                    o_ref[lo, :, c * LANES:(c + 1) * LANES] = stage_ref[c, pl.ds(lo, rows // RUN, stride=RUN), :]
        else:
            o_ref[...] = done
        acc_prev[sl, :] = jnp.zeros((rows, acc_prev.shape[1]), F32)
        return done

    def tile_step(acc_cur, acc_prev):
        side_work()
        done = finish_rows(acc_prev)
        word = pltpu.bitcast(jnp.max(done, axis=0, keepdims=True)[:, 0:LANES], jnp.uint32)
        zero = ((word >> 16) >> 16).astype(F32)
        h = jnp.maximum(jnp.dot(xb_ref[...], w1_ref[...], preferred_element_type=F32), 0.0)
        h = h * h + zero[:, 0:1]
        acc_cur[...] += jnp.dot(h.astype(BF16), w2_ref[...], preferred_element_type=F32)

    @pl.when((i < n_tiles) & (lax.rem(i, 2) == 0))
    def _():
        tile_step(acc0_ref, acc1_ref)

    @pl.when((i < n_tiles) & (lax.rem(i, 2) == 1))
    def _():
        tile_step(acc1_ref, acc0_ref)

    @pl.when((i == n_tiles) & (lax.rem(i, 2) == 0))
    def _():
        finish_rows(acc1_ref)

    @pl.when((i == n_tiles) & (lax.rem(i, 2) == 1))
    def _():
        finish_rows(acc0_ref)


def _ffn(x, xb, w1, w2, ln_g, ln_b, layer, cast_jobs=(), natural_out=False):
    t, d = x.shape
    d_ff = w1.shape[1]
    n_i, n_j = t // FFN_TM, d_ff // FFN_TF
    rows = FFN_TM // n_j
    assert rows % RUN == 0 and SUPER % rows == 0
    per_super = SUPER // rows
    run = rows // RUN

    def finished(i, j):
        return (jnp.where(i == 0, 0, (i - 1) * n_j + j), 0)

    def w_tile(i, j):
        return jnp.where(i == n_i, n_j - 1, j)

    if natural_out:
        def finished_natural(i, j):
            k = finished(i, j)[0]
            return (k // per_super, 0, k % per_super, 0, 0)
        out_spec = pl.BlockSpec((None, RUN, None, run, d), finished_natural)
        out_shape = jax.ShapeDtypeStruct((t // SUPER, RUN, per_super, run, d), F32)
    else:
        out_spec = pl.BlockSpec((rows, d), finished)
        out_shape = jax.ShapeDtypeStruct((t, d), F32)

    c_in, c_out, c_shape = _cast_specs(cast_jobs, lambda i, j: i * n_j + j)
    out, *casts = pl.pallas_call(
        _with_casts(functools.partial(_ffn_kernel, natural_out=natural_out), 6, 1, len(cast_jobs)),
        grid=(n_i + 1, n_j),
        in_specs=[pl.BlockSpec((rows, d), finished),
                  pl.BlockSpec((FFN_TM, d), lambda i, j: (jnp.minimum(i, n_i - 1), 0)),
                  pl.BlockSpec((d, FFN_TF), lambda i, j: (0, w_tile(i, j))),
                  pl.BlockSpec((FFN_TF, d), lambda i, j: (w_tile(i, j), 0)),
                  pl.BlockSpec((1, 1, d), lambda i, j: (layer, 0, 0)),
                  pl.BlockSpec((1, 1, d), lambda i, j: (layer, 0, 0))] + c_in,
        out_specs=[out_spec] + c_out,
        out_shape=[out_shape] + c_shape,
        scratch_shapes=[pltpu.VMEM((FFN_TM, d), F32), pltpu.VMEM((FFN_TM, d), F32),
                        pltpu.VMEM((d // LANES, rows, LANES), F32)],
        compiler_params=_params("arbitrary", "arbitrary"),
        name="ffn",
    )(x, xb, w1, w2, ln_g, ln_b, *[job[0] for job in cast_jobs])
    return out.reshape(t, d), casts


def _gmlp_bias_in_working_order(b_s):
    n_layers, groups, chunk = b_s.shape
    half = chunk // RUN
    reps = SUPER // chunk
    b = b_s.reshape(n_layers, groups, half, RUN).transpose(0, 3, 2, 1)
    b = jnp.broadcast_to(b[:, :, None], (n_layers, RUN, reps, half, groups))
    return b.reshape(n_layers, SUPER, groups)


def kernel(x, mem, w_in_a, w_in_b, w_s, b_s, vnorm_g, vnorm_b, w_mem_kv, w_out, ln1_g, ln1_b, w_ff1, w_ff2,
           ln2_g, ln2_b):
    batch, seq, d = x.shape
    row3 = lambda a: a.reshape(a.shape[0], 1, a.shape[1])
    ln1_g, ln1_b, ln2_g, ln2_b = row3(ln1_g), row3(ln1_b), row3(ln2_g), row3(ln2_b)
    vnorm_g, vnorm_b = row3(vnorm_g), row3(vnorm_b)
    bs_perm = _gmlp_bias_in_working_order(b_s)
    a_cols = w_in_a.shape[2] - MEM_WIDTH
    b_cols = w_in_b.shape[2] - MEM_WIDTH

    def cast_job(kind, layer, n_blocks):
        if kind == "in":
            return (w_in_a if layer % 2 == 0 else w_in_b, layer // 2, n_blocks)
        return ({"out": w_out, "ff1": w_ff1, "ff2": w_ff2}[kind], layer, n_blocks)

    hosted = {("memkv", 0): [("in", 0, 8)],
              ("proj", 0): [("out", 0, 48), ("ff1", 0, 64), ("ff2", 0, 64), ("in", 1, 64), ("out", 1, 48),
                            ("in", 2, 64)],
              ("ffn", 0): [("ff1", 1, 64), ("ff2", 1, 64)],
              ("ffn", 1): [("ff1", 2, 64), ("ff2", 2, 64)],
              ("proj", 2): [("out", 2, 48), ("out", 3, 48), ("in", 3, 64)],
              ("ffn", 2): [("ff1", 3, 64), ("ff2", 3, 64)]}
    bf16 = {}

    def jobs_of(host):
        return [cast_job(*job) for job in hosted.get(host, [])]

    def collect(host, casts):
        bf16.update(zip([job[:2] for job in hosted.get(host, [])], casts))

    xf = _perm(x.reshape(batch * seq, d))
    kv, kt, casts = _memkv(mem, w_mem_kv, cast_jobs=jobs_of(("memkv", 0)))
    collect(("memkv", 0), casts)
    for i in range(DEPTH):
        w_in = bf16["in", i]
        if i % 2 == 0:
            qkv, casts = _proj(xf, w_in, a_cols, gelu=False, cast_jobs=jobs_of(("proj", i)))
            collect(("proj", i), casts)
            outs, lses = [], []
            for g, (window, dilation) in enumerate(A_PAIRS):
                o, lse = _attn_group(qkv, batch, seq, g, window, dilation)
                outs.append(o)
                lses.append(lse)
            xf, xb = _mix_a(xf, outs, lses, seq, w_in, kv, kt, bf16["out", i], ln1_g, ln1_b, i)
        else:
            z, _ = _proj(xf, w_in, b_cols, gelu=True)
            xf, xb = _mix_b(xf, z, seq, w_s, bs_perm, vnorm_g, vnorm_b, i // 2, w_in, kv, kt, bf16["out", i],
                            ln1_g, ln1_b, i)
        xf, casts = _ffn(xf, xb, bf16["ff1", i], bf16["ff2", i], ln2_g, ln2_b, i, cast_jobs=jobs_of(("ffn", i)),
                         natural_out=(i == DEPTH - 1))
        collect(("ffn", i), casts)
    return xf.reshape(batch, seq, d)
```
